```python
import jax, jax.numpy as jnp
from jax import lax
import numpy as np

D_MODEL = 2048
BATCH = 4
SEQ = 4096
DEPTH = 4
DEC_BATCH = 4
DEC_SEQ = 8192
PAST_LEN = 128

HEAD_DIM = 128
DILATIONS = ((128, 1), (512, 4), (2048, 16))
A_HEADS_PER_GROUP = 4
A_HEADS = A_HEADS_PER_GROUP * len(DILATIONS)
B_Q_HEADS = D_MODEL // 256
B_KV_HEADS = 2
B_HALF_WINDOW = 128
B_BLOCK = 128
C_HEADS = D_MODEL // HEAD_DIM
GRID_W = 64
NA_ROWS = 8
NA_COLS = 16
NUM_BUCKETS = 32
MAX_DISTANCE = 1024
D_FF = -(-8 * D_MODEL // 768) * 256
RMS_EPS = 1e-6
A_IN = len(DILATIONS) * 3 * A_HEADS_PER_GROUP * HEAD_DIM
B_IN = (B_Q_HEADS + 2 * B_KV_HEADS) * HEAD_DIM
AB_IN = A_IN + B_IN
AB_OUT = (A_HEADS_PER_GROUP + B_Q_HEADS) * HEAD_DIM
C_IN = 3 * C_HEADS * HEAD_DIM
C_OUT = C_HEADS * HEAD_DIM
SCALE = HEAD_DIM ** -0.5
NEG_INF = -1e30

kernel_name = "hybrid_dilated_window_neighbourhood_encoder"


def rms_norm(x, g):
    xf = x.astype(jnp.float32)
    y = xf * lax.rsqrt(jnp.mean(xf * xf, axis=-1, keepdims=True) + RMS_EPS)
    return (y * g.astype(jnp.float32)).astype(x.dtype)


def t5_bucket(rel):
    nb = NUM_BUCKETS // 2
    max_exact = nb // 2
    ret = (rel > 0).astype(np.int32) * nb
    n = np.abs(rel)
    large = max_exact + (np.log(np.maximum(n, 1) / max_exact) / np.log(MAX_DISTANCE / max_exact)
                         * (nb - max_exact)).astype(np.int32)
    large = np.minimum(large, nb - 1)
    return (ret + np.where(n < max_exact, n, large)).astype(np.int32)


def t5_bias(table_cols, blk, d):
    rel = (np.arange(3 * blk)[None, :] - blk - np.arange(blk)[:, None]) * d
    return table_cols[t5_bucket(rel)].transpose(2, 0, 1).astype(jnp.float32)


def banded_attention(q, k, v, half_w, blk, bias, sink=None):
    n, L, H, dh = q.shape
    G = k.shape[2]
    rep = H // G
    nb = -(-L // blk)
    Lp = nb * blk
    qb = jnp.pad(q, ((0, 0), (0, Lp - L), (0, 0), (0, 0))).reshape(n, nb, blk, G, rep, dh)

    def windows(t):
        tp = jnp.pad(t, ((0, 0), (blk, Lp - L + blk), (0, 0), (0, 0))).reshape(n, nb + 2, blk, G, dh)
        return jnp.concatenate([tp[:, :-2], tp[:, 1:-1], tp[:, 2:]], axis=2)

    kb, vb = windows(k), windows(v)
    qpos = np.arange(Lp).reshape(nb, blk)
    kpos = np.arange(-blk, Lp + blk).reshape(nb + 2, blk)
    kpos = np.concatenate([kpos[:-2], kpos[1:-1], kpos[2:]], axis=1)
    rel = kpos[:, None, :] - qpos[:, :, None]
    mask = (np.abs(rel) <= half_w) & (kpos[:, None, :] >= 0) & (kpos[:, None, :] < L)

    s = jnp.einsum('nbqgrd,nbkgd->nbgrqk', qb, kb, preferred_element_type=jnp.float32) * SCALE
    s = s + bias.reshape(G, rep, blk, 3 * blk)[None, None]
    s = jnp.where(mask[None, :, None, None], s, NEG_INF)
    m = jnp.max(s, axis=-1)
    if sink is not None:
        sk = sink.astype(jnp.float32).reshape(G, rep, 1)
        m = jnp.maximum(m, sk)
    p = jnp.exp(s - m[..., None])
    den = jnp.sum(p, axis=-1)
    if sink is not None:
        den = den + jnp.exp(sk - m)
    o = jnp.einsum('nbgrqk,nbkgd->nbqgrd', p.astype(v.dtype), vb, preferred_element_type=jnp.float32)
    den_t = jnp.moveaxis(den, -1, 2)
    lse = jnp.moveaxis(m, -1, 2) + jnp.log(den_t)
    o = (o / den_t[..., None]).reshape(n, Lp, H, dh)[:, :L]
    lse = lse.reshape(n, Lp, H)[:, :L]
    return o, lse


def dilated_attention(q, k, v, d, half, bias):
    Bn, T, H, dh = q.shape

    def split(t):
        return t.reshape(Bn, T // d, d, H, dh).transpose(0, 2, 1, 3, 4).reshape(Bn * d, T // d, H, dh)

    o, lse = banded_attention(split(q), split(k), split(v), half, half, bias)
    o = o.reshape(Bn, d, T // d, H, dh).transpose(0, 2, 1, 3, 4).reshape(Bn, T, H, dh)
    lse = lse.reshape(Bn, d, T // d, H).transpose(0, 2, 1, 3).reshape(Bn, T, H)
    return o, lse


def mixer_ab(h, w_in, w_out, sink, t5_table):
    Bn, T, _ = h.shape
    proj = h @ w_in
    a = proj[..., :A_IN].reshape(Bn, T, len(DILATIONS), 3, A_HEADS_PER_GROUP, HEAD_DIM)
    b = proj[..., A_IN:]
    nq = B_Q_HEADS * HEAD_DIM
    nkv = B_KV_HEADS * HEAD_DIM
    bq = b[..., :nq].reshape(Bn, T, B_Q_HEADS, HEAD_DIM)
    bk = b[..., nq:nq + nkv].reshape(Bn, T, B_KV_HEADS, HEAD_DIM)
    bv = b[..., nq + nkv:].reshape(Bn, T, B_KV_HEADS, HEAD_DIM)
    outs, lses = [], []
    for g, (w, d) in enumerate(DILATIONS):
        half = w // (2 * d)
        cols = t5_table[:, g * A_HEADS_PER_GROUP:(g + 1) * A_HEADS_PER_GROUP]
        o, l = dilated_attention(a[:, :, g, 0], a[:, :, g, 1], a[:, :, g, 2], d, half, t5_bias(cols, half, d))
        outs.append(o)
        lses.append(l)
    wts = jax.nn.softmax(jnp.stack(lses), axis=0)
    o_a = jnp.sum(wts[..., None] * jnp.stack(outs), axis=0)
    o_b, _ = banded_attention(bq, bk, bv, B_HALF_WINDOW, B_BLOCK,
                              t5_bias(t5_table[:, A_HEADS:], B_BLOCK, 1), sink)
    mixed = jnp.concatenate([o_a.reshape(Bn, T, -1), o_b.reshape(Bn, T, -1)], axis=-1).astype(h.dtype)
    return mixed @ w_out


def neighbourhood_attention(q, k, v, rpb):
    Bn, T, H, dh = q.shape
    rows = T // GRID_W
    kh = min(NA_ROWS, rows)
    r = np.arange(rows)
    rs = np.clip(r - kh // 2, 0, rows - kh)
    key_rows = rs[:, None] + np.arange(kh)[None, :]
    c = np.arange(GRID_W)
    cs = np.clip(c - NA_COLS // 2, 0, GRID_W - NA_COLS)
    col_mask = (c[None, :] >= cs[:, None]) & (c[None, :] < cs[:, None] + NA_COLS)
    dr = key_rows - r[:, None] + NA_ROWS - 1
    dc = np.clip(c[None, :] - c[:, None] + NA_COLS - 1, 0, 2 * NA_COLS - 2)
    bias = rpb[:, dr[:, None, :, None], dc[None, :, None, :]].astype(jnp.float32)
    bias = bias.transpose(1, 0, 2, 3, 4)
    qg = q.reshape(Bn, rows, GRID_W, H, dh)
    kg = k.reshape(Bn, rows, GRID_W, H, dh)[:, key_rows]
    vg = v.reshape(Bn, rows, GRID_W, H, dh)[:, key_rows]
    s = jnp.einsum('brqhd,brikhd->brhqik', qg, kg, preferred_element_type=jnp.float32) * SCALE + bias[None]
    s = jnp.where(col_mask[:, None, :], s, NEG_INF)
    p = jax.nn.softmax(s.reshape(Bn, rows, H, GRID_W, kh * GRID_W), axis=-1).reshape(s.shape)
    o = jnp.einsum('brhqik,brikhd->brqhd', p.astype(v.dtype), vg, preferred_element_type=jnp.float32)
    return o.reshape(Bn, T, H, dh)


def mixer_c(h, w_in, w_out, rpb):
    Bn, T, _ = h.shape
    qkv = (h @ w_in).reshape(Bn, T, 3, C_HEADS, HEAD_DIM)
    o = neighbourhood_attention(qkv[:, :, 0], qkv[:, :, 1], qkv[:, :, 2], rpb)
    return o.reshape(Bn, T, C_OUT).astype(h.dtype) @ w_out


def swiglu(h, w_gate, w_up, w_down):
    return (jax.nn.silu(h @ w_gate) * (h @ w_up)) @ w_down


def trunk(x, w_in_ab, w_out_ab, sink_b, w_in_c, w_out_c, rpb_c, t5_table,
          norm_mix, norm_ffn, w_gate, w_up, w_down, norm_final):
    for layer in range(DEPTH):
        j = layer // 2
        h = rms_norm(x, norm_mix[layer])
        if layer % 2 == 0:
            x = x + mixer_ab(h, w_in_ab[j], w_out_ab[j], sink_b[j], t5_table)
        else:
            x = x + mixer_c(h, w_in_c[j], w_out_c[j], rpb_c[j])
        x = x + swiglu(rms_norm(x, norm_ffn[layer]), w_gate[layer], w_up[layer], w_down[layer])
    return rms_norm(x, norm_final)


def setup_inputs(seed: int = 0) -> dict:
    key = jax.random.key(seed)
    ks = jax.random.split(key, 17)
    n_even = (DEPTH + 1) // 2
    n_odd = DEPTH // 2
    nrm = jax.random.normal
    f32 = jnp.float32
    return {
        'x_prompt': nrm(ks[0], (BATCH, SEQ, D_MODEL), f32),
        'x_sample': nrm(ks[1], (DEC_BATCH, DEC_SEQ, D_MODEL), f32),
        'w_in_ab': nrm(ks[2], (n_even, D_MODEL, AB_IN), f32) * D_MODEL ** -0.5,
        'w_out_ab': nrm(ks[3], (n_even, AB_OUT, D_MODEL), f32) * AB_OUT ** -0.5,
        'sink_b': 0.5 * nrm(ks[4], (n_even, B_Q_HEADS), f32),
        'w_in_c': nrm(ks[5], (n_odd, D_MODEL, C_IN), f32) * D_MODEL ** -0.5,
        'w_out_c': nrm(ks[6], (n_odd, C_OUT, D_MODEL), f32) * C_OUT ** -0.5,
        'rpb_c': 0.2 * nrm(ks[7], (n_odd, C_HEADS, 2 * NA_ROWS - 1, 2 * NA_COLS - 1), f32),
        't5_table': 0.2 * nrm(ks[8], (NUM_BUCKETS, A_HEADS + B_Q_HEADS), f32),
        'norm_mix': 1.0 + 0.05 * nrm(ks[9], (DEPTH, D_MODEL), f32),
        'norm_ffn': 1.0 + 0.05 * nrm(ks[10], (DEPTH, D_MODEL), f32),
        'w_gate': nrm(ks[11], (DEPTH, D_MODEL, D_FF), f32) * D_MODEL ** -0.5,
        'w_up': nrm(ks[12], (DEPTH, D_MODEL, D_FF), f32) * D_MODEL ** -0.5,
        'w_down': nrm(ks[13], (DEPTH, D_FF, D_MODEL), f32) * D_FF ** -0.5,
        'norm_final': 1.0 + 0.05 * nrm(ks[14], (D_MODEL,), f32),
    }


def reference(x_prompt, x_sample, w_in_ab, w_out_ab, sink_b, w_in_c, w_out_c, rpb_c, t5_table,
              norm_mix, norm_ffn, w_gate, w_up, w_down, norm_final):
    y_prompt = trunk(x_prompt, w_in_ab, w_out_ab, sink_b, w_in_c, w_out_c, rpb_c, t5_table,
                     norm_mix, norm_ffn, w_gate, w_up, w_down, norm_final)
    y_sample = trunk(x_sample, w_in_ab, w_out_ab, sink_b, w_in_c, w_out_c, rpb_c, t5_table,
                     norm_mix, norm_ffn, w_gate, w_up, w_down, norm_final)
    return (y_prompt, y_sample)
```

```python
import functools

import numpy as np
import jax
import jax.numpy as jnp
from jax import lax
from jax.experimental import pallas as pl
from jax.experimental.pallas import tpu as pltpu

D_MODEL = 2048
DEPTH = 4
HEAD_DIM = 128
DILATIONS = ((128, 1), (512, 4), (2048, 16))
A_HEADS_PER_GROUP = 4
A_HEADS = A_HEADS_PER_GROUP * len(DILATIONS)
B_Q_HEADS = 8
B_KV_HEADS = 2
B_REP = B_Q_HEADS // B_KV_HEADS
B_HALF_WINDOW = 128
C_HEADS = 16
GRID_W = 64
NA_ROWS = 8
NA_COLS = 16
NUM_BUCKETS = 32
MAX_DISTANCE = 1024
RMS_EPS = 1e-6
A_GROUP_IN = 3 * A_HEADS_PER_GROUP * HEAD_DIM
A_IN = len(DILATIONS) * A_GROUP_IN
B_IN = (B_Q_HEADS + 2 * B_KV_HEADS) * HEAD_DIM
AB_IN = A_IN + B_IN
A_OUT = A_HEADS_PER_GROUP * HEAD_DIM
B_OUT = B_Q_HEADS * HEAD_DIM
AB_OUT = A_OUT + B_OUT
C_IN = 3 * C_HEADS * HEAD_DIM
C_OUT = C_HEADS * HEAD_DIM
SCALE = HEAD_DIM ** -0.5
NEG_INF = -1e30

A_HALF = 64
A_SUB = 128
A_KEYS = A_SUB + 2 * A_HALF
B_SUB = 128
B_KEYS = B_SUB + 2 * B_HALF_WINDOW
C_QROWS = 4
C_QTOK = C_QROWS * GRID_W
C_KBLKS = 3
C_HG = 4

VMEM_LIMIT = 52 * 1024 * 1024

f32 = jnp.float32
bf16 = jnp.bfloat16


def _cparams(sem):
    return pltpu.CompilerParams(dimension_semantics=sem, vmem_limit_bytes=VMEM_LIMIT)


def _resident(shape, index_map):
    return pl.BlockSpec(shape, index_map, pipeline_mode=pl.Buffered(1))


def _rms_rows(x_ref, g_ref, dst_ref, rows, chunk=64):
    def body(c, carry):
        r = pl.multiple_of(c * chunk, chunk)
        x = x_ref[pl.ds(r, chunk), :]
        ms = jnp.mean(x * x, axis=-1, keepdims=True)
        dst_ref[pl.ds(r, chunk), :] = ((x * lax.rsqrt(ms + RMS_EPS)) * g_ref[...]).astype(dst_ref.dtype)
        return carry
    lax.fori_loop(0, rows // chunk, body, 0)


def _norm_matmul_kernel(x_ref, g_ref, w_ref, o_ref, h_ref, *, tm):
    @pl.when(pl.program_id(1) == 0)
    def _():
        _rms_rows(x_ref, g_ref, h_ref, tm)
    o_ref[...] = jnp.dot(h_ref[...], w_ref[...], preferred_element_type=f32).astype(o_ref.dtype)


def _norm_matmul(x, g, w, *, tm=1024, tn=1024):
    m, d = x.shape
    n = w.shape[1]
    return pl.pallas_call(
        functools.partial(_norm_matmul_kernel, tm=tm),
        grid=(m // tm, n // tn),
        in_specs=[pl.BlockSpec((tm, d), lambda i, j: (i, 0)),
                  _resident((1, d), lambda i, j: (0, 0)),
                  pl.BlockSpec((d, tn), lambda i, j: (0, j))],
        out_specs=pl.BlockSpec((tm, tn), lambda i, j: (i, j)),
        out_shape=jax.ShapeDtypeStruct((m, n), bf16),
        scratch_shapes=[pltpu.VMEM((tm, d), bf16)],
        compiler_params=_cparams(("parallel", "arbitrary")),
        name="norm_proj",
    )(x, g.reshape(1, d), w)


def _ffn_kernel(x_ref, g_ref, wg_ref, wu_ref, wd_ref, *rest, tm, nf, final):
    if final:
        gf_ref, o_ref, h_ref = rest
    else:
        o_ref, h_ref = rest
    f = pl.program_id(1)

    @pl.when(f == 0)
    def _():
        _rms_rows(x_ref, g_ref, h_ref, tm)
        o_ref[...] = x_ref[...]

    h = h_ref[...]
    gate = jnp.dot(h, wg_ref[...], preferred_element_type=f32)
    up = jnp.dot(h, wu_ref[...], preferred_element_type=f32)
    act = (gate * jax.nn.sigmoid(gate) * up).astype(bf16)
    o_ref[...] += jnp.dot(act, wd_ref[...], preferred_element_type=f32)

    if final:
        @pl.when(f == nf - 1)
        def _():
            _rms_rows(o_ref, gf_ref, o_ref, tm)


def _ffn(x, g, wg, wu, wd, g_final=None, *, tm=512, tf=512):
    m, d = x.shape
    dff = wg.shape[1]
    nf = dff // tf
    final = g_final is not None
    in_specs = [pl.BlockSpec((tm, d), lambda i, f: (i, 0)),
                _resident((1, d), lambda i, f: (0, 0)),
                pl.BlockSpec((d, tf), lambda i, f: (0, f)),
                pl.BlockSpec((d, tf), lambda i, f: (0, f)),
                pl.BlockSpec((tf, d), lambda i, f: (f, 0))]
    args = [x, g.reshape(1, d), wg, wu, wd]
    if final:
        in_specs.append(_resident((1, d), lambda i, f: (0, 0)))
        args.append(g_final.reshape(1, d))
    return pl.pallas_call(
        functools.partial(_ffn_kernel, tm=tm, nf=nf, final=final),
        grid=(m // tm, nf),
        in_specs=in_specs,
        out_specs=pl.BlockSpec((tm, d), lambda i, f: (i, 0)),
        out_shape=jax.ShapeDtypeStruct((m, d), f32),
        scratch_shapes=[pltpu.VMEM((tm, d), bf16)],
        compiler_params=_cparams(("parallel", "arbitrary")),
        name="ffn",
    )(*args)


def _t5_bucket(rel):
    nb = NUM_BUCKETS // 2
    max_exact = nb // 2
    ret = (rel > 0).astype(np.int32) * nb
    n = np.abs(rel)
    large = max_exact + (np.log(np.maximum(n, 1) / max_exact) / np.log(MAX_DISTANCE / max_exact)
                         * (nb - max_exact)).astype(np.int32)
    large = np.minimum(large, nb - 1)
    return (ret + np.where(n < max_exact, n, large)).astype(np.int32)


def _band_bias(table_cols, n_q, half, dilation):
    rel = np.arange(n_q + 2 * half)[None, :] - half - np.arange(n_q)[:, None]
    band = np.abs(rel) <= half
    vals = table_cols[_t5_bucket(rel * dilation)]
    vals = jnp.where(band[:, :, None], vals.astype(f32), NEG_INF)
    return vals.transpose(2, 0, 1)


def _attn_a_kernel(q_ref, kp_ref, kc_ref, kn_ref, vp_ref, vc_ref, vn_ref, bias_ref, o_ref,
                   kbuf, vbuf, *, tq, seq_len):
    i = pl.program_id(2)
    kbuf[0:A_HALF] = kp_ref[0]
    kbuf[A_HALF:A_HALF + tq] = kc_ref[0]
    kbuf[A_HALF + tq:] = kn_ref[0]
    vbuf[0:A_HALF] = vp_ref[0]
    vbuf[A_HALF:A_HALF + tq] = vc_ref[0]
    vbuf[A_HALF + tq:] = vn_ref[0]
    for sb in range(tq // A_SUB):
        r0 = sb * A_SUB
        kpos = i * tq + (r0 - A_HALF) + lax.broadcasted_iota(jnp.int32, (1, A_KEYS), 1)
        valid = jnp.where(kpos >= 0, kpos, seq_len) < seq_len
        for h in range(A_HEADS_PER_GROUP):
            c0 = h * HEAD_DIM
            q = q_ref[0, r0:r0 + A_SUB, c0:c0 + HEAD_DIM]
            k = kbuf[r0:r0 + A_KEYS, c0:c0 + HEAD_DIM]
            v = vbuf[r0:r0 + A_KEYS, c0:c0 + HEAD_DIM]
            s = lax.dot_general(q, k, (((1,), (1,)), ((), ())), preferred_element_type=f32)
            s = s * SCALE + bias_ref[h]
            s = jnp.where(valid, s, NEG_INF)
            m = jnp.max(s, axis=-1, keepdims=True)
            p = jnp.exp(s - m)
            den = jnp.sum(p, axis=-1, keepdims=True)
            o = jnp.dot(p.astype(bf16), v, preferred_element_type=f32) / den
            lse = m + jnp.log(den)
            o_ref[0, r0:r0 + A_SUB, c0:c0 + HEAD_DIM] = o
            o_ref[0, r0:r0 + A_SUB, A_OUT + c0:A_OUT + c0 + HEAD_DIM] = jnp.broadcast_to(lse, (A_SUB, HEAD_DIM))


def _attn_a_group(proj, bias, g, dilation):
    bn, t, _ = proj.shape
    L = t // dilation
    tq = min(256, L)
    nblk = L // tq
    hb = tq // A_HALF
    n_halo = L // A_HALF
    cb = AB_IN // A_OUT
    pv = proj.reshape(bn, L, dilation * AB_IN)
    qc, kc, vc = 3 * g, 3 * g + 1, 3 * g + 2

    def cur(c):
        return pl.BlockSpec((1, tq, A_OUT), lambda b, r, i: (b, i, r * cb + c))

    def prev(c):
        return pl.BlockSpec((1, A_HALF, A_OUT), lambda b, r, i: (b, jnp.maximum(i * hb - 1, 0), r * cb + c))

    def nxt(c):
        return pl.BlockSpec((1, A_HALF, A_OUT), lambda b, r, i: (b, jnp.minimum((i + 1) * hb, n_halo - 1), r * cb + c))

    out = pl.pallas_call(
        functools.partial(_attn_a_kernel, tq=tq, seq_len=L),
        grid=(bn, dilation, nblk),
        in_specs=[cur(qc), prev(kc), cur(kc), nxt(kc), prev(vc), cur(vc), nxt(vc),
                  _resident((A_HEADS_PER_GROUP, A_SUB, A_KEYS), lambda b, r, i: (0, 0, 0))],
        out_specs=pl.BlockSpec((1, tq, 2 * A_OUT), lambda b, r, i: (b, i, r)),
        out_shape=jax.ShapeDtypeStruct((bn, L, dilation * 2 * A_OUT), f32),
        scratch_shapes=[pltpu.VMEM((tq + 2 * A_HALF, A_OUT), bf16),
                        pltpu.VMEM((tq + 2 * A_HALF, A_OUT), bf16)],
        compiler_params=_cparams(("parallel", "parallel", "arbitrary")),
        name=f"attn_a{g}",
    )(pv, pv, pv, pv, pv, pv, pv, bias)
    return out.reshape(bn * t, 2 * A_OUT)


def _attn_b_kernel(sink_ref, cur_ref, hp_ref, hn_ref, bias_ref, o_ref, kbuf, vbuf, *, tq, seq_len):
    i = pl.program_id(1)
    kw = B_KV_HEADS * HEAD_DIM
    k_off = B_OUT
    v_off = B_OUT + kw
    hw = B_HALF_WINDOW
    kbuf[0:hw] = hp_ref[0, :, 0:kw]
    kbuf[hw:hw + tq] = cur_ref[0, :, k_off:k_off + kw]
    kbuf[hw + tq:] = hn_ref[0, :, 0:kw]
    vbuf[0:hw] = hp_ref[0, :, kw:2 * kw]
    vbuf[hw:hw + tq] = cur_ref[0, :, v_off:v_off + kw]
    vbuf[hw + tq:] = hn_ref[0, :, kw:2 * kw]
    for sb in range(tq // B_SUB):
        r0 = sb * B_SUB
        kpos = i * tq + (r0 - hw) + lax.broadcasted_iota(jnp.int32, (1, B_KEYS), 1)
        valid = jnp.where(kpos >= 0, kpos, seq_len) < seq_len
        for g in range(B_KV_HEADS):
            qs = jnp.concatenate(
                [cur_ref[0, r0:r0 + B_SUB, (g * B_REP + hh) * HEAD_DIM:(g * B_REP + hh + 1) * HEAD_DIM]
                 for hh in range(B_REP)], axis=0)
            k = kbuf[r0:r0 + B_KEYS, g * HEAD_DIM:(g + 1) * HEAD_DIM]
            v = vbuf[r0:r0 + B_KEYS, g * HEAD_DIM:(g + 1) * HEAD_DIM]
            s_all = lax.dot_general(qs, k, (((1,), (1,)), ((), ())), preferred_element_type=f32)
            s_all = s_all * SCALE + bias_ref[g]
            s_all = jnp.where(valid, s_all, NEG_INF)
            ps, dens = [], []
            for hh in range(B_REP):
                s = s_all[hh * B_SUB:(hh + 1) * B_SUB]
                sk = sink_ref[g * B_REP + hh]
                m = jnp.maximum(jnp.max(s, axis=-1, keepdims=True), sk)
                p = jnp.exp(s - m)
                dens.append(jnp.sum(p, axis=-1, keepdims=True) + jnp.exp(sk - m))
                ps.append(p.astype(bf16))
            o_all = jnp.dot(jnp.concatenate(ps, axis=0), v, preferred_element_type=f32)
            for hh in range(B_REP):
                c0 = (g * B_REP + hh) * HEAD_DIM
                o = o_all[hh * B_SUB:(hh + 1) * B_SUB] / dens[hh]
                o_ref[0, r0:r0 + B_SUB, c0:c0 + HEAD_DIM] = o.astype(o_ref.dtype)


def _attn_b(proj, bias, sink):
    bn, t, _ = proj.shape
    tq = 256
    nblk = t // tq
    hb = tq // B_HALF_WINDOW
    n_halo = t // B_HALF_WINDOW
    kvw = 2 * B_KV_HEADS * HEAD_DIM
    out = pl.pallas_call(
        functools.partial(_attn_b_kernel, tq=tq, seq_len=t),
        grid=(bn, nblk),
        in_specs=[pl.BlockSpec(memory_space=pltpu.SMEM),
                  pl.BlockSpec((1, tq, B_IN), lambda b, i: (b, i, A_IN // B_IN)),
                  pl.BlockSpec((1, B_HALF_WINDOW, kvw), lambda b, i: (b, jnp.maximum(i * hb - 1, 0), AB_IN // kvw - 1)),
                  pl.BlockSpec((1, B_HALF_WINDOW, kvw), lambda b, i: (b, jnp.minimum((i + 1) * hb, n_halo - 1), AB_IN // kvw - 1)),
                  _resident((B_KV_HEADS, B_REP * B_SUB, B_KEYS), lambda b, i: (0, 0, 0))],
        out_specs=pl.BlockSpec((1, tq, B_OUT), lambda b, i: (b, i, 0)),
        out_shape=jax.ShapeDtypeStruct((bn, t, B_OUT), bf16),
        scratch_shapes=[pltpu.VMEM((tq + 2 * B_HALF_WINDOW, B_KV_HEADS * HEAD_DIM), bf16),
                        pltpu.VMEM((tq + 2 * B_HALF_WINDOW, B_KV_HEADS * HEAD_DIM), bf16)],
        compiler_params=_cparams(("parallel", "arbitrary")),
        name="attn_b",
    )(sink, proj, proj, proj, bias)
    return out.reshape(bn * t, B_OUT)


def _out_ab_kernel(x_ref, a0_ref, a1_ref, a2_ref, ob_ref, w_ref, o_ref, mix_ref):
    a_refs = (a0_ref, a1_ref, a2_ref)
    for h in range(A_HEADS_PER_GROUP):
        c0 = h * HEAD_DIM
        lses = [r[:, A_OUT + c0:A_OUT + c0 + HEAD_DIM] for r in a_refs]
        mx = jnp.maximum(jnp.maximum(lses[0], lses[1]), lses[2])
        es = [jnp.exp(l - mx) for l in lses]
        tot = es[0] + es[1] + es[2]
        acc = (es[0] / tot) * a_refs[0][:, c0:c0 + HEAD_DIM]
        acc += (es[1] / tot) * a_refs[1][:, c0:c0 + HEAD_DIM]
        acc += (es[2] / tot) * a_refs[2][:, c0:c0 + HEAD_DIM]
        mix_ref[:, c0:c0 + HEAD_DIM] = acc.astype(bf16)
    mix_ref[:, A_OUT:] = ob_ref[...]
    o_ref[...] = x_ref[...] + jnp.dot(mix_ref[...], w_ref[...], preferred_element_type=f32)


def _out_ab(x, a_outs, o_b, w, *, tm=256):
    m, d = x.shape
    row = lambda i: (i, 0)
    return pl.pallas_call(
        _out_ab_kernel,
        grid=(m // tm,),
        in_specs=[pl.BlockSpec((tm, d), row),
                  pl.BlockSpec((tm, 2 * A_OUT), row),
                  pl.BlockSpec((tm, 2 * A_OUT), row),
                  pl.BlockSpec((tm, 2 * A_OUT), row),
                  pl.BlockSpec((tm, B_OUT), row),
                  _resident((AB_OUT, d), lambda i: (0, 0))],
        out_specs=pl.BlockSpec((tm, d), row),
        out_shape=jax.ShapeDtypeStruct((m, d), f32),
        scratch_shapes=[pltpu.VMEM((tm, AB_OUT), bf16)],
        compiler_params=_cparams(("parallel",)),
        name="out_ab",
    )(x, *a_outs, o_b, w)


def _na_bias(rpb):
    n_krows = C_KBLKS * C_QROWS
    qi = np.arange(C_QROWS)[:, None]
    kj = np.arange(n_krows)[None, :]
    c = np.arange(GRID_W)
    cs = np.clip(c - NA_COLS // 2, 0, GRID_W - NA_COLS)
    col_ok = (c[None, :] >= cs[:, None]) & (c[None, :] < cs[:, None] + NA_COLS)
    dc = np.clip(c[None, :] - c[:, None] + NA_COLS - 1, 0, 2 * NA_COLS - 2)
    out = []
    for case in range(3):
        delta = case * C_QROWS
        first = (np.zeros_like(qi), qi, np.full_like(qi, C_QROWS))[case]
        row_ok = (kj >= first) & (kj < first + NA_ROWS)
        dr = np.clip(kj - qi - delta + NA_ROWS - 1, 0, 2 * NA_ROWS - 2)
        ok = row_ok[:, None, :, None] & col_ok[None, :, None, :]
        dr_f = np.broadcast_to(dr[:, None, :, None], ok.shape)
        dc_f = np.broadcast_to(dc[None, :, None, :], ok.shape)
        vals = rpb[:, dr_f, dc_f].astype(f32)
        vals = jnp.where(ok[None], vals, NEG_INF)
        out.append(vals.reshape(rpb.shape[0], C_QTOK, n_krows * GRID_W))
    return jnp.stack(out)


def _attn_c_kernel(q_ref, k0_ref, k1_ref, k2_ref, v0_ref, v1_ref, v2_ref, bias_ref, o_ref):
    k_refs = (k0_ref, k1_ref, k2_ref)
    v_refs = (v0_ref, v1_ref, v2_ref)
    for h in range(C_HG):
        c0 = h * HEAD_DIM
        q = q_ref[0, :, c0:c0 + HEAD_DIM]
        ss = []
        for j in range(C_KBLKS):
            s = lax.dot_general(q, k_refs[j][0, :, c0:c0 + HEAD_DIM], (((1,), (1,)), ((), ())),
                                preferred_element_type=f32)
            ss.append(s * SCALE + bias_ref[0, h, :, j * C_QTOK:(j + 1) * C_QTOK])
        m = jnp.max(ss[0], axis=-1, keepdims=True)
        for j in range(1, C_KBLKS):
            m = jnp.maximum(m, jnp.max(ss[j], axis=-1, keepdims=True))
        den = None
        o = None
        for j in range(C_KBLKS):
            p = jnp.exp(ss[j] - m)
            dj = jnp.sum(p, axis=-1, keepdims=True)
            oj = jnp.dot(p.astype(bf16), v_refs[j][0, :, c0:c0 + HEAD_DIM], preferred_element_type=f32)
            den = dj if den is None else den + dj
            o = oj if o is None else o + oj
        o_ref[0, :, c0:c0 + HEAD_DIM] = (o / den).astype(o_ref.dtype)


def _attn_c(qkv, bias):
    bn, t, _ = qkv.shape
    nblk = t // C_QTOK
    hgw = C_HG * HEAD_DIM
    n_hg = C_HEADS // C_HG
    k_cb = C_OUT // hgw
    v_cb = 2 * C_OUT // hgw

    def kv_spec(cb, j):
        return pl.BlockSpec((1, C_QTOK, hgw),
                            lambda g, b, i: (b, jnp.clip(i - 1, 0, nblk - C_KBLKS) + j, cb + g))

    def case(i):
        return jnp.where(i == 0, 0, jnp.where(i == nblk - 1, 2, 1))

    out = pl.pallas_call(
        _attn_c_kernel,
        grid=(n_hg, bn, nblk),
        in_specs=[pl.BlockSpec((1, C_QTOK, hgw), lambda g, b, i: (b, i, g)),
                  kv_spec(k_cb, 0), kv_spec(k_cb, 1), kv_spec(k_cb, 2),
                  kv_spec(v_cb, 0), kv_spec(v_cb, 1), kv_spec(v_cb, 2),
                  pl.BlockSpec((1, C_HG, C_QTOK, C_KBLKS * C_QTOK), lambda g, b, i: (case(i), g, 0, 0))],
        out_specs=pl.BlockSpec((1, C_QTOK, hgw), lambda g, b, i: (b, i, g)),
        out_shape=jax.ShapeDtypeStruct((bn, t, C_OUT), bf16),
        compiler_params=_cparams(("parallel", "parallel", "arbitrary")),
        name="attn_c",
    )(qkv, qkv, qkv, qkv, qkv, qkv, qkv, bias)
    return out.reshape(bn * t, C_OUT)


def _out_proj_kernel(x_ref, a_ref, w_ref, o_ref):
    o_ref[...] = x_ref[...] + jnp.dot(a_ref[...], w_ref[...], preferred_element_type=f32)


def _out_proj(x, a, w, *, tm=512):
    m, d = x.shape
    k = a.shape[1]
    row = lambda i: (i, 0)
    return pl.pallas_call(
        _out_proj_kernel,
        grid=(m // tm,),
        in_specs=[pl.BlockSpec((tm, d), row),
                  pl.BlockSpec((tm, k), row),
                  _resident((k, d), lambda i: (0, 0))],
        out_specs=pl.BlockSpec((tm, d), row),
        out_shape=jax.ShapeDtypeStruct((m, d), f32),
        compiler_params=_cparams(("parallel",)),
        name="out_c",
    )(x, a, w)


def _mixer_ab(x, bn, t, g_norm, w_in, w_out, sink, a_biases, b_bias):
    proj = _norm_matmul(x, g_norm, w_in).reshape(bn, t, AB_IN)
    a_outs = [_attn_a_group(proj, a_biases[g], g, d) for g, (_, d) in enumerate(DILATIONS)]
    o_b = _attn_b(proj, b_bias, sink)
    return _out_ab(x, a_outs, o_b, w_out)


def _mixer_c(x, bn, t, g_norm, w_in, w_out, c_bias):
    qkv = _norm_matmul(x, g_norm, w_in).reshape(bn, t, C_IN)
    return _out_proj(x, _attn_c(qkv, c_bias), w_out)


def _trunk(x3, w_in_ab, w_out_ab, sink_b, w_in_c, w_out_c, a_biases, b_bias, c_biases,
           norm_mix, norm_ffn, w_gate, w_up, w_down, norm_final):
    bn, t, d = x3.shape
    x = x3.reshape(bn * t, d)
    for layer in range(DEPTH):
        j = layer // 2
        if layer % 2 == 0:
            x = _mixer_ab(x, bn, t, norm_mix[layer], w_in_ab[j], w_out_ab[j], sink_b[j], a_biases, b_bias)
        else:
            x = _mixer_c(x, bn, t, norm_mix[layer], w_in_c[j], w_out_c[j], c_biases[j])
        x = _ffn(x, norm_ffn[layer], w_gate[layer], w_up[layer], w_down[layer],
                 norm_final if layer == DEPTH - 1 else None)
    return x.reshape(bn, t, d)


def kernel(x_prompt, x_sample, w_in_ab, w_out_ab, sink_b, w_in_c, w_out_c, rpb_c, t5_table, norm_mix, norm_ffn, w_gate, w_up, w_down, norm_final):
    a_biases = [_band_bias(t5_table[:, g * A_HEADS_PER_GROUP:(g + 1) * A_HEADS_PER_GROUP], A_SUB, A_HALF, d)
                for g, (_, d) in enumerate(DILATIONS)]
    b_bias = _band_bias(t5_table[:, A_HEADS:], B_SUB, B_HALF_WINDOW, 1)
    b_bias = b_bias.reshape(B_KV_HEADS, B_REP * B_SUB, B_KEYS)
    c_biases = [_na_bias(rpb_c[j]) for j in range(rpb_c.shape[0])]
    weights = (w_in_ab.astype(bf16), w_out_ab.astype(bf16), sink_b, w_in_c.astype(bf16), w_out_c.astype(bf16),
               a_biases, b_bias, c_biases, norm_mix, norm_ffn,
               w_gate.astype(bf16), w_up.astype(bf16), w_down.astype(bf16), norm_final)
    return (_trunk(x_prompt, *weights), _trunk(x_sample, *weights))
```

```python
import functools

import numpy as np
import jax
import jax.numpy as jnp
from jax import lax
from jax.experimental import pallas as pl
from jax.experimental.pallas import tpu as pltpu

D_MODEL = 2048
DEPTH = 4
HEAD_DIM = 128
DILATIONS = ((128, 1), (512, 4), (2048, 16))
A_HEADS_PER_GROUP = 4
A_HEADS = A_HEADS_PER_GROUP * len(DILATIONS)
B_Q_HEADS = 8
B_KV_HEADS = 2
B_REP = B_Q_HEADS // B_KV_HEADS
B_HALF_WINDOW = 128
C_HEADS = 16
GRID_W = 64
NA_ROWS = 8
NA_COLS = 16
NUM_BUCKETS = 32
MAX_DISTANCE = 1024
RMS_EPS = 1e-6
A_GROUP_IN = 3 * A_HEADS_PER_GROUP * HEAD_DIM
A_IN = len(DILATIONS) * A_GROUP_IN
B_IN = (B_Q_HEADS + 2 * B_KV_HEADS) * HEAD_DIM
AB_IN = A_IN + B_IN
A_OUT = A_HEADS_PER_GROUP * HEAD_DIM
B_OUT = B_Q_HEADS * HEAD_DIM
AB_OUT = A_OUT + B_OUT
C_IN = 3 * C_HEADS * HEAD_DIM
C_OUT = C_HEADS * HEAD_DIM
SCALE = HEAD_DIM ** -0.5
NEG_INF = -1e30

A_HALF = 64
A_SUB = 128
A_KEYS = A_SUB + 2 * A_HALF
B_SUB = 128
B_KEYS = B_SUB + 2 * B_HALF_WINDOW
C_QROWS = 4
C_QTOK = C_QROWS * GRID_W
C_KBLKS = 3
C_HG = 4

VMEM_LIMIT = 52 * 1024 * 1024

f32 = jnp.float32
bf16 = jnp.bfloat16


def _cparams(sem):
    return pltpu.CompilerParams(dimension_semantics=sem, vmem_limit_bytes=VMEM_LIMIT)


def _resident(shape, index_map):
    return pl.BlockSpec(shape, index_map, pipeline_mode=pl.Buffered(1))


def _rms_rows(x_ref, g_ref, dst_ref, rows, chunk=64):
    def body(c, carry):
        r = pl.multiple_of(c * chunk, chunk)
        x = x_ref[pl.ds(r, chunk), :]
        ms = jnp.mean(x * x, axis=-1, keepdims=True)
        dst_ref[pl.ds(r, chunk), :] = ((x * lax.rsqrt(ms + RMS_EPS)) * g_ref[...]).astype(dst_ref.dtype)
        return carry
    lax.fori_loop(0, rows // chunk, body, 0)


def _norm_matmul_kernel(x_ref, g_ref, w_ref, o_ref, h_ref, *, tm):
    @pl.when(pl.program_id(1) == 0)
    def _():
        _rms_rows(x_ref, g_ref, h_ref, tm)
    o_ref[...] = jnp.dot(h_ref[...], w_ref[...], preferred_element_type=f32).astype(o_ref.dtype)


def _norm_matmul(x, g, w, *, tm=1024, tn=1024):
    m, d = x.shape
    n = w.shape[1]
    return pl.pallas_call(
        functools.partial(_norm_matmul_kernel, tm=tm),
        grid=(m // tm, n // tn),
        in_specs=[pl.BlockSpec((tm, d), lambda i, j: (i, 0)),
                  _resident((1, d), lambda i, j: (0, 0)),
                  pl.BlockSpec((d, tn), lambda i, j: (0, j))],
        out_specs=pl.BlockSpec((tm, tn), lambda i, j: (i, j)),
        out_shape=jax.ShapeDtypeStruct((m, n), bf16),
        scratch_shapes=[pltpu.VMEM((tm, d), bf16)],
        compiler_params=_cparams(("parallel", "arbitrary")),
        name="norm_proj",
    )(x, g.reshape(1, d), w)


def _ffn_kernel(x_ref, g_ref, wg_ref, wu_ref, wd_ref, *rest, tm, nf, final):
    if final:
        gf_ref, o_ref, h_ref = rest
    else:
        o_ref, h_ref = rest
    f = pl.program_id(1)

    @pl.when(f == 0)
    def _():
        _rms_rows(x_ref, g_ref, h_ref, tm)
        o_ref[...] = x_ref[...]

    h = h_ref[...]
    gate = jnp.dot(h, wg_ref[...], preferred_element_type=f32)
    up = jnp.dot(h, wu_ref[...], preferred_element_type=f32)
    act = (gate * jax.nn.sigmoid(gate) * up).astype(bf16)
    o_ref[...] += jnp.dot(act, wd_ref[...], preferred_element_type=f32)

    if final:
        @pl.when(f == nf - 1)
        def _():
            _rms_rows(o_ref, gf_ref, o_ref, tm)


def _ffn(x, g, wg, wu, wd, g_final=None, *, tm=512, tf=512):
    m, d = x.shape
    dff = wg.shape[1]
    nf = dff // tf
    final = g_final is not None
    in_specs = [pl.BlockSpec((tm, d), lambda i, f: (i, 0)),
                _resident((1, d), lambda i, f: (0, 0)),
                pl.BlockSpec((d, tf), lambda i, f: (0, f)),
                pl.BlockSpec((d, tf), lambda i, f: (0, f)),
                pl.BlockSpec((tf, d), lambda i, f: (f, 0))]
    args = [x, g.reshape(1, d), wg, wu, wd]
    if final:
        in_specs.append(_resident((1, d), lambda i, f: (0, 0)))
        args.append(g_final.reshape(1, d))
    return pl.pallas_call(
        functools.partial(_ffn_kernel, tm=tm, nf=nf, final=final),
        grid=(m // tm, nf),
        in_specs=in_specs,
        out_specs=pl.BlockSpec((tm, d), lambda i, f: (i, 0)),
        out_shape=jax.ShapeDtypeStruct((m, d), f32),
        scratch_shapes=[pltpu.VMEM((tm, d), bf16)],
        compiler_params=_cparams(("parallel", "arbitrary")),
        name="ffn",
    )(*args)


def _t5_bucket(rel):
    nb = NUM_BUCKETS // 2
    max_exact = nb // 2
    ret = (rel > 0).astype(np.int32) * nb
    n = np.abs(rel)
    large = max_exact + (np.log(np.maximum(n, 1) / max_exact) / np.log(MAX_DISTANCE / max_exact)
                         * (nb - max_exact)).astype(np.int32)
    large = np.minimum(large, nb - 1)
    return (ret + np.where(n < max_exact, n, large)).astype(np.int32)


def _band_bias(table_cols, n_q, half, dilation):
    rel = np.arange(n_q + 2 * half)[None, :] - half - np.arange(n_q)[:, None]
    band = np.abs(rel) <= half
    vals = table_cols[_t5_bucket(rel * dilation)]
    vals = jnp.where(band[:, :, None], vals.astype(f32), NEG_INF)
    return vals.transpose(2, 0, 1)


def _attn_a_kernel(q_ref, kp_ref, kc_ref, kn_ref, vp_ref, vc_ref, vn_ref, bias_ref, o_ref,
                   kbuf, vbuf, *, tq, seq_len):
    i = pl.program_id(2)
    kbuf[0:A_HALF] = kp_ref[0]
    kbuf[A_HALF:A_HALF + tq] = kc_ref[0]
    kbuf[A_HALF + tq:] = kn_ref[0]
    vbuf[0:A_HALF] = vp_ref[0]
    vbuf[A_HALF:A_HALF + tq] = vc_ref[0]
    vbuf[A_HALF + tq:] = vn_ref[0]
    for sb in range(tq // A_SUB):
        r0 = sb * A_SUB
        kpos = i * tq + (r0 - A_HALF) + lax.broadcasted_iota(jnp.int32, (1, A_KEYS), 1)
        valid = jnp.where(kpos >= 0, kpos, seq_len) < seq_len
        for h in range(A_HEADS_PER_GROUP):
            c0 = h * HEAD_DIM
            q = q_ref[0, r0:r0 + A_SUB, c0:c0 + HEAD_DIM]
            k = kbuf[r0:r0 + A_KEYS, c0:c0 + HEAD_DIM]
            v = vbuf[r0:r0 + A_KEYS, c0:c0 + HEAD_DIM]
            s = lax.dot_general(q, k, (((1,), (1,)), ((), ())), preferred_element_type=f32)
            s = s * SCALE + bias_ref[h]
            s = jnp.where(valid, s, NEG_INF)
            m = jnp.max(s, axis=-1, keepdims=True)
            p = jnp.exp(s - m)
            den = jnp.sum(p, axis=-1, keepdims=True)
            o = jnp.dot(p.astype(bf16), v, preferred_element_type=f32) / den
            lse = m + jnp.log(den)
            o_ref[0, r0:r0 + A_SUB, c0:c0 + HEAD_DIM] = o
            o_ref[0, r0:r0 + A_SUB, A_OUT + c0:A_OUT + c0 + HEAD_DIM] = jnp.broadcast_to(lse, (A_SUB, HEAD_DIM))


def _attn_a_group(proj, bias, g, dilation):
    bn, t, _ = proj.shape
    L = t // dilation
    tq = min(256, L)
    nblk = L // tq
    hb = tq // A_HALF
    n_halo = L // A_HALF
    cb = AB_IN // A_OUT
    pv = proj.reshape(bn, L, dilation * AB_IN)
    qc, kc, vc = 3 * g, 3 * g + 1, 3 * g + 2

    def cur(c):
        return pl.BlockSpec((1, tq, A_OUT), lambda b, r, i: (b, i, r * cb + c))

    def prev(c):
        return pl.BlockSpec((1, A_HALF, A_OUT), lambda b, r, i: (b, jnp.maximum(i * hb - 1, 0), r * cb + c))

    def nxt(c):
        return pl.BlockSpec((1, A_HALF, A_OUT), lambda b, r, i: (b, jnp.minimum((i + 1) * hb, n_halo - 1), r * cb + c))

    out = pl.pallas_call(
        functools.partial(_attn_a_kernel, tq=tq, seq_len=L),
        grid=(bn, dilation, nblk),
        in_specs=[cur(qc), prev(kc), cur(kc), nxt(kc), prev(vc), cur(vc), nxt(vc),
                  _resident((A_HEADS_PER_GROUP, A_SUB, A_KEYS), lambda b, r, i: (0, 0, 0))],
        out_specs=pl.BlockSpec((1, tq, 2 * A_OUT), lambda b, r, i: (b, i, r)),
        out_shape=jax.ShapeDtypeStruct((bn, L, dilation * 2 * A_OUT), f32),
        scratch_shapes=[pltpu.VMEM((tq + 2 * A_HALF, A_OUT), bf16),
                        pltpu.VMEM((tq + 2 * A_HALF, A_OUT), bf16)],
        compiler_params=_cparams(("parallel", "parallel", "arbitrary")),
        name=f"attn_a{g}",
    )(pv, pv, pv, pv, pv, pv, pv, bias)
    return out.reshape(bn * t, 2 * A_OUT)


def _attn_b_kernel(sink_ref, cur_ref, hp_ref, hn_ref, bias_ref, o_ref, kbuf, vbuf, *, tq, seq_len):
    i = pl.program_id(1)
    kw = B_KV_HEADS * HEAD_DIM
    k_off = B_OUT
    v_off = B_OUT + kw
    hw = B_HALF_WINDOW
    kbuf[0:hw] = hp_ref[0, :, 0:kw]
    kbuf[hw:hw + tq] = cur_ref[0, :, k_off:k_off + kw]
    kbuf[hw + tq:] = hn_ref[0, :, 0:kw]
    vbuf[0:hw] = hp_ref[0, :, kw:2 * kw]
    vbuf[hw:hw + tq] = cur_ref[0, :, v_off:v_off + kw]
    vbuf[hw + tq:] = hn_ref[0, :, kw:2 * kw]
    for sb in range(tq // B_SUB):
        r0 = sb * B_SUB
        kpos = i * tq + (r0 - hw) + lax.broadcasted_iota(jnp.int32, (1, B_KEYS), 1)
        valid = jnp.where(kpos >= 0, kpos, seq_len) < seq_len
        for g in range(B_KV_HEADS):
            qs = jnp.concatenate(
                [cur_ref[0, r0:r0 + B_SUB, (g * B_REP + hh) * HEAD_DIM:(g * B_REP + hh + 1) * HEAD_DIM]
                 for hh in range(B_REP)], axis=0)
            k = kbuf[r0:r0 + B_KEYS, g * HEAD_DIM:(g + 1) * HEAD_DIM]
            v = vbuf[r0:r0 + B_KEYS, g * HEAD_DIM:(g + 1) * HEAD_DIM]
            s_all = lax.dot_general(qs, k, (((1,), (1,)), ((), ())), preferred_element_type=f32)
            s_all = s_all * SCALE + bias_ref[g]
            s_all = jnp.where(valid, s_all, NEG_INF)
            ps, dens = [], []
            for hh in range(B_REP):
                s = s_all[hh * B_SUB:(hh + 1) * B_SUB]
                sk = sink_ref[g * B_REP + hh]
                m = jnp.maximum(jnp.max(s, axis=-1, keepdims=True), sk)
                p = jnp.exp(s - m)
                dens.append(jnp.sum(p, axis=-1, keepdims=True) + jnp.exp(sk - m))
                ps.append(p.astype(bf16))
            o_all = jnp.dot(jnp.concatenate(ps, axis=0), v, preferred_element_type=f32)
            for hh in range(B_REP):
                c0 = (g * B_REP + hh) * HEAD_DIM
                o = o_all[hh * B_SUB:(hh + 1) * B_SUB] / dens[hh]
                o_ref[0, r0:r0 + B_SUB, c0:c0 + HEAD_DIM] = o.astype(o_ref.dtype)


def _attn_b(proj, bias, sink):
    bn, t, _ = proj.shape
    tq = 256
    nblk = t // tq
    hb = tq // B_HALF_WINDOW
    n_halo = t // B_HALF_WINDOW
    kvw = 2 * B_KV_HEADS * HEAD_DIM
    out = pl.pallas_call(
        functools.partial(_attn_b_kernel, tq=tq, seq_len=t),
        grid=(bn, nblk),
        in_specs=[pl.BlockSpec(memory_space=pltpu.SMEM),
                  pl.BlockSpec((1, tq, B_IN), lambda b, i: (b, i, A_IN // B_IN)),
                  pl.BlockSpec((1, B_HALF_WINDOW, kvw), lambda b, i: (b, jnp.maximum(i * hb - 1, 0), AB_IN // kvw - 1)),
                  pl.BlockSpec((1, B_HALF_WINDOW, kvw), lambda b, i: (b, jnp.minimum((i + 1) * hb, n_halo - 1), AB_IN // kvw - 1)),
                  _resident((B_KV_HEADS, B_REP * B_SUB, B_KEYS), lambda b, i: (0, 0, 0))],
        out_specs=pl.BlockSpec((1, tq, B_OUT), lambda b, i: (b, i, 0)),
        out_shape=jax.ShapeDtypeStruct((bn, t, B_OUT), bf16),
        scratch_shapes=[pltpu.VMEM((tq + 2 * B_HALF_WINDOW, B_KV_HEADS * HEAD_DIM), bf16),
                        pltpu.VMEM((tq + 2 * B_HALF_WINDOW, B_KV_HEADS * HEAD_DIM), bf16)],
        compiler_params=_cparams(("parallel", "arbitrary")),
        name="attn_b",
    )(sink, proj, proj, proj, bias)
    return out.reshape(bn * t, B_OUT)


def _out_ab_kernel(x_ref, a0_ref, a1_ref, a2_ref, ob_ref, w_ref, o_ref, mix_ref):
    a_refs = (a0_ref, a1_ref, a2_ref)
    for h in range(A_HEADS_PER_GROUP):
        c0 = h * HEAD_DIM
        lses = [r[:, A_OUT + c0:A_OUT + c0 + HEAD_DIM] for r in a_refs]
        mx = jnp.maximum(jnp.maximum(lses[0], lses[1]), lses[2])
        es = [jnp.exp(l - mx) for l in lses]
        tot = es[0] + es[1] + es[2]
        acc = (es[0] / tot) * a_refs[0][:, c0:c0 + HEAD_DIM]
        acc += (es[1] / tot) * a_refs[1][:, c0:c0 + HEAD_DIM]
        acc += (es[2] / tot) * a_refs[2][:, c0:c0 + HEAD_DIM]
        mix_ref[:, c0:c0 + HEAD_DIM] = acc.astype(bf16)
    mix_ref[:, A_OUT:] = ob_ref[...]
    o_ref[...] = x_ref[...] + jnp.dot(mix_ref[...], w_ref[...], preferred_element_type=f32)


def _out_ab(x, a_outs, o_b, w, *, tm=256):
    m, d = x.shape
    row = lambda i: (i, 0)
    return pl.pallas_call(
        _out_ab_kernel,
        grid=(m // tm,),
        in_specs=[pl.BlockSpec((tm, d), row),
                  pl.BlockSpec((tm, 2 * A_OUT), row),
                  pl.BlockSpec((tm, 2 * A_OUT), row),
                  pl.BlockSpec((tm, 2 * A_OUT), row),
                  pl.BlockSpec((tm, B_OUT), row),
                  _resident((AB_OUT, d), lambda i: (0, 0))],
        out_specs=pl.BlockSpec((tm, d), row),
        out_shape=jax.ShapeDtypeStruct((m, d), f32),
        scratch_shapes=[pltpu.VMEM((tm, AB_OUT), bf16)],
        compiler_params=_cparams(("parallel",)),
        name="out_ab",
    )(x, *a_outs, o_b, w)


def _na_bias(rpb):
    n_heads = rpb.shape[0]
    n_krows = C_KBLKS * C_QROWS
    c = np.arange(GRID_W)
    cs = np.clip(c - NA_COLS // 2, 0, GRID_W - NA_COLS)
    col_ok = (c[None, :] >= cs[:, None]) & (c[None, :] < cs[:, None] + NA_COLS)
    dc = np.clip(c[None, :] - c[:, None] + NA_COLS - 1, 0, 2 * NA_COLS - 2)
    onehot = (dc[None] == np.arange(2 * NA_COLS - 1)[:, None, None]).astype(np.float32)
    tiles = jnp.einsum('hdc,cqk->hqdk', rpb.astype(f32), onehot, precision=lax.Precision.HIGHEST)
    tiles = jnp.where(col_ok[None, :, None, :], tiles, NEG_INF)
    out = []
    for case in range(3):
        rows = []
        for qi in range(C_QROWS):
            first = (0, qi, C_QROWS)[case]
            dr0 = first - qi - case * C_QROWS + NA_ROWS - 1
            band = tiles[:, :, dr0:dr0 + NA_ROWS, :]
            rows.append(jnp.pad(band, ((0, 0), (0, 0), (first, n_krows - NA_ROWS - first), (0, 0)),
                                constant_values=NEG_INF))
        out.append(jnp.stack(rows, axis=1).reshape(n_heads, C_QTOK, n_krows * GRID_W))
    return jnp.stack(out)


def _attn_c_kernel(q_ref, k0_ref, k1_ref, k2_ref, v0_ref, v1_ref, v2_ref, bias_ref, o_ref):
    k_refs = (k0_ref, k1_ref, k2_ref)
    v_refs = (v0_ref, v1_ref, v2_ref)
    for h in range(C_HG):
        c0 = h * HEAD_DIM
        q = q_ref[0, :, c0:c0 + HEAD_DIM]
        ss = []
        for j in range(C_KBLKS):
            s = lax.dot_general(q, k_refs[j][0, :, c0:c0 + HEAD_DIM], (((1,), (1,)), ((), ())),
                                preferred_element_type=f32)
            ss.append(s * SCALE + bias_ref[0, h, :, j * C_QTOK:(j + 1) * C_QTOK])
        m = jnp.max(ss[0], axis=-1, keepdims=True)
        for j in range(1, C_KBLKS):
            m = jnp.maximum(m, jnp.max(ss[j], axis=-1, keepdims=True))
        den = None
        o = None
        for j in range(C_KBLKS):
            p = jnp.exp(ss[j] - m)
            dj = jnp.sum(p, axis=-1, keepdims=True)
            oj = jnp.dot(p.astype(bf16), v_refs[j][0, :, c0:c0 + HEAD_DIM], preferred_element_type=f32)
            den = dj if den is None else den + dj
            o = oj if o is None else o + oj
        o_ref[0, :, c0:c0 + HEAD_DIM] = (o / den).astype(o_ref.dtype)


def _attn_c(qkv, bias):
    bn, t, _ = qkv.shape
    nblk = t // C_QTOK
    hgw = C_HG * HEAD_DIM
    n_hg = C_HEADS // C_HG
    k_cb = C_OUT // hgw
    v_cb = 2 * C_OUT // hgw

    def kv_spec(cb, j):
        return pl.BlockSpec((1, C_QTOK, hgw),
                            lambda g, b, i: (b, jnp.clip(i - 1, 0, nblk - C_KBLKS) + j, cb + g))

    def case(i):
        return jnp.where(i == 0, 0, jnp.where(i == nblk - 1, 2, 1))

    out = pl.pallas_call(
        _attn_c_kernel,
        grid=(n_hg, bn, nblk),
        in_specs=[pl.BlockSpec((1, C_QTOK, hgw), lambda g, b, i: (b, i, g)),
                  kv_spec(k_cb, 0), kv_spec(k_cb, 1), kv_spec(k_cb, 2),
                  kv_spec(v_cb, 0), kv_spec(v_cb, 1), kv_spec(v_cb, 2),
                  pl.BlockSpec((1, C_HG, C_QTOK, C_KBLKS * C_QTOK), lambda g, b, i: (case(i), g, 0, 0))],
        out_specs=pl.BlockSpec((1, C_QTOK, hgw), lambda g, b, i: (b, i, g)),
        out_shape=jax.ShapeDtypeStruct((bn, t, C_OUT), bf16),
        compiler_params=_cparams(("parallel", "parallel", "arbitrary")),
        name="attn_c",
    )(qkv, qkv, qkv, qkv, qkv, qkv, qkv, bias)
    return out.reshape(bn * t, C_OUT)


def _out_proj_kernel(x_ref, a_ref, w_ref, o_ref):
    o_ref[...] = x_ref[...] + jnp.dot(a_ref[...], w_ref[...], preferred_element_type=f32)


def _out_proj(x, a, w, *, tm=512):
    m, d = x.shape
    k = a.shape[1]
    row = lambda i: (i, 0)
    return pl.pallas_call(
        _out_proj_kernel,
        grid=(m // tm,),
        in_specs=[pl.BlockSpec((tm, d), row),
                  pl.BlockSpec((tm, k), row),
                  _resident((k, d), lambda i: (0, 0))],
        out_specs=pl.BlockSpec((tm, d), row),
        out_shape=jax.ShapeDtypeStruct((m, d), f32),
        compiler_params=_cparams(("parallel",)),
        name="out_c",
    )(x, a, w)


def _mixer_ab(x, bn, t, g_norm, w_in, w_out, sink, a_biases, b_bias):
    proj = _norm_matmul(x, g_norm, w_in).reshape(bn, t, AB_IN)
    a_outs = [_attn_a_group(proj, a_biases[g], g, d) for g, (_, d) in enumerate(DILATIONS)]
    o_b = _attn_b(proj, b_bias, sink)
    return _out_ab(x, a_outs, o_b, w_out)


def _mixer_c(x, bn, t, g_norm, w_in, w_out, c_bias):
    qkv = _norm_matmul(x, g_norm, w_in).reshape(bn, t, C_IN)
    return _out_proj(x, _attn_c(qkv, c_bias), w_out)


def _trunk(x3, w_in_ab, w_out_ab, sink_b, w_in_c, w_out_c, a_biases, b_bias, c_biases,
           norm_mix, norm_ffn, w_gate, w_up, w_down, norm_final):
    bn, t, d = x3.shape
    x = x3.reshape(bn * t, d)
    for layer in range(DEPTH):
        j = layer // 2
        if layer % 2 == 0:
            x = _mixer_ab(x, bn, t, norm_mix[layer], w_in_ab[j], w_out_ab[j], sink_b[j], a_biases, b_bias)
        else:
            x = _mixer_c(x, bn, t, norm_mix[layer], w_in_c[j], w_out_c[j], c_biases[j])
        x = _ffn(x, norm_ffn[layer], w_gate[layer], w_up[layer], w_down[layer],
                 norm_final if layer == DEPTH - 1 else None)
    return x.reshape(bn, t, d)


def kernel(x_prompt, x_sample, w_in_ab, w_out_ab, sink_b, w_in_c, w_out_c, rpb_c, t5_table, norm_mix, norm_ffn, w_gate, w_up, w_down, norm_final):
    a_biases = [_band_bias(t5_table[:, g * A_HEADS_PER_GROUP:(g + 1) * A_HEADS_PER_GROUP], A_SUB, A_HALF, d)
                for g, (_, d) in enumerate(DILATIONS)]
    b_bias = _band_bias(t5_table[:, A_HEADS:], B_SUB, B_HALF_WINDOW, 1)
    b_bias = b_bias.reshape(B_KV_HEADS, B_REP * B_SUB, B_KEYS)
    c_biases = [_na_bias(rpb_c[j]) for j in range(rpb_c.shape[0])]
    weights = (w_in_ab.astype(bf16), w_out_ab.astype(bf16), sink_b, w_in_c.astype(bf16), w_out_c.astype(bf16),
               a_biases, b_bias, c_biases, norm_mix, norm_ffn,
               w_gate.astype(bf16), w_up.astype(bf16), w_down.astype(bf16), norm_final)
    return (_trunk(x_prompt, *weights), _trunk(x_sample, *weights))
```

```python
import functools

import numpy as np
import jax
import jax.numpy as jnp
from jax import lax
from jax.experimental import pallas as pl
from jax.experimental.pallas import tpu as pltpu

D_MODEL = 2048
DEPTH = 4
HEAD_DIM = 128
DILATIONS = ((128, 1), (512, 4), (2048, 16))
A_HEADS_PER_GROUP = 4
A_HEADS = A_HEADS_PER_GROUP * len(DILATIONS)
B_Q_HEADS = 8
B_KV_HEADS = 2
B_REP = B_Q_HEADS // B_KV_HEADS
B_HALF_WINDOW = 128
C_HEADS = 16
GRID_W = 64
NA_ROWS = 8
NA_COLS = 16
NUM_BUCKETS = 32
MAX_DISTANCE = 1024
RMS_EPS = 1e-6
A_GROUP_IN = 3 * A_HEADS_PER_GROUP * HEAD_DIM
A_IN = len(DILATIONS) * A_GROUP_IN
B_IN = (B_Q_HEADS + 2 * B_KV_HEADS) * HEAD_DIM
AB_IN = A_IN + B_IN
N_AB_SLABS = AB_IN // A_GROUP_IN
A_OUT = A_HEADS_PER_GROUP * HEAD_DIM
A_OUT_W = A_OUT + HEAD_DIM
B_OUT = B_Q_HEADS * HEAD_DIM
AB_OUT = A_OUT + B_OUT
C_IN = 3 * C_HEADS * HEAD_DIM
C_OUT = C_HEADS * HEAD_DIM
SCALE = HEAD_DIM ** -0.5
NEG_INF = -1e30

A_HALF = 64
A_CHUNK = 1024
A_SUB = 128
MXU_WIDTH = 256
PERM_PIECE = 256
B_SUB = 128
B_KEYS = B_SUB + 2 * B_HALF_WINDOW
C_QROWS = 4
C_QTOK = C_QROWS * GRID_W
C_KBLKS = 3
C_HG = 4

VMEM_LIMIT = 52 * 1024 * 1024
FFN_VMEM_LIMIT = 58 * 1024 * 1024

f32 = jnp.float32
bf16 = jnp.bfloat16


def _cparams(sem, vmem_limit=VMEM_LIMIT):
    return pltpu.CompilerParams(dimension_semantics=sem, vmem_limit_bytes=vmem_limit)


def _resident(shape, index_map):
    return pl.BlockSpec(shape, index_map, pipeline_mode=pl.Buffered(1))


def _rms_rows(x_ref, g_ref, dst_ref, rows, chunk=64):
    def body(c, carry):
        r = pl.multiple_of(c * chunk, chunk)
        x = x_ref[pl.ds(r, chunk), :]
        ms = jnp.mean(x * x, axis=-1, keepdims=True)
        dst_ref[pl.ds(r, chunk), :] = ((x * lax.rsqrt(ms + RMS_EPS)) * g_ref[...]).astype(dst_ref.dtype)
        return carry
    lax.fori_loop(0, rows // chunk, body, 0, unroll=2)


def _norm_matmul_kernel(x_ref, g_ref, w_ref, o_ref, h_ref, *, tm):
    @pl.when(pl.program_id(1) == 0)
    def _():
        _rms_rows(x_ref, g_ref, h_ref, tm)
    o_ref[...] = jnp.dot(h_ref[...], w_ref[...], preferred_element_type=f32).astype(o_ref.dtype)


def _norm_matmul(x, g, w, *, tm=1024, tn=1024):
    m, d = x.shape
    n = w.shape[1]
    return pl.pallas_call(
        functools.partial(_norm_matmul_kernel, tm=tm),
        grid=(m // tm, n // tn),
        in_specs=[pl.BlockSpec((tm, d), lambda i, j: (i, 0)),
                  _resident((1, d), lambda i, j: (0, 0)),
                  pl.BlockSpec((d, tn), lambda i, j: (0, j))],
        out_specs=pl.BlockSpec((tm, tn), lambda i, j: (i, j)),
        out_shape=jax.ShapeDtypeStruct((m, n), bf16),
        scratch_shapes=[pltpu.VMEM((tm, d), bf16)],
        compiler_params=_cparams(("parallel", "arbitrary")),
        name="norm_proj",
    )(x, g.reshape(1, d), w)


def _deinterleave_perm(d):
    per = PERM_PIECE // d
    p = np.zeros((PERM_PIECE, PERM_PIECE), np.float32)
    for r in range(d):
        for v in range(per):
            p[r * per + v, v * d + r] = 1.0
    return p


def _norm_proj_ab_kernel(x_ref, g_ref, w_ref, perm_ref, o_ref, h_ref, *, tm, steps_per_slab):
    j = pl.program_id(1)

    @pl.when(j == 0)
    def _():
        _rms_rows(x_ref, g_ref, h_ref, tm)

    t = jnp.dot(h_ref[...], w_ref[...], preferred_element_type=f32).astype(bf16)
    slab = j // steps_per_slab

    @pl.when(jnp.logical_or(slab == 0, slab == N_AB_SLABS - 1))
    def _():
        o_ref[0] = t

    for g, (_, d) in enumerate(DILATIONS):
        if d == 1:
            continue

        @pl.when(slab == g)
        def _(g=g, d=d):
            run = tm // d
            per = PERM_PIECE // d
            for p in range(tm // PERM_PIECE):
                piece = jnp.dot(perm_ref[g - 1], t[p * PERM_PIECE:(p + 1) * PERM_PIECE, :],
                                preferred_element_type=f32).astype(bf16)
                for r in range(d):
                    o_ref[0, r * run + p * per:r * run + (p + 1) * per, :] = piece[r * per:(r + 1) * per, :]


def _norm_proj_ab(x, g, w, perms, *, tn=768):
    m, d = x.shape
    tm = A_CHUNK
    steps_per_slab = A_GROUP_IN // tn
    return pl.pallas_call(
        functools.partial(_norm_proj_ab_kernel, tm=tm, steps_per_slab=steps_per_slab),
        grid=(m // tm, AB_IN // tn),
        in_specs=[pl.BlockSpec((tm, d), lambda i, j: (i, 0)),
                  _resident((1, d), lambda i, j: (0, 0)),
                  pl.BlockSpec((d, tn), lambda i, j: (0, j)),
                  _resident(perms.shape, lambda i, j: (0, 0, 0))],
        out_specs=pl.BlockSpec((1, tm, tn), lambda i, j: (j // steps_per_slab, i, j % steps_per_slab)),
        out_shape=jax.ShapeDtypeStruct((N_AB_SLABS, m, A_GROUP_IN), bf16),
        scratch_shapes=[pltpu.VMEM((tm, d), bf16)],
        compiler_params=_cparams(("parallel", "arbitrary")),
        name="norm_proj_ab",
    )(x, g.reshape(1, d), w, perms)


def _ffn_kernel(x_ref, g_ref, wg_ref, wu_ref, wd_ref, *rest, tm, nf, final):
    if final:
        gf_ref, o_ref, h_ref = rest
    else:
        o_ref, h_ref = rest
    f = pl.program_id(1)

    @pl.when(f == 0)
    def _():
        _rms_rows(x_ref, g_ref, h_ref, tm)
        o_ref[...] = x_ref[...]

    h = h_ref[...]
    gate = jnp.dot(h, wg_ref[...], preferred_element_type=f32)
    up = jnp.dot(h, wu_ref[...], preferred_element_type=f32)
    act = (gate * jax.nn.sigmoid(gate) * up).astype(bf16)
    o_ref[...] += jnp.dot(act, wd_ref[...], preferred_element_type=f32)

    if final:
        @pl.when(f == nf - 1)
        def _():
            _rms_rows(o_ref, gf_ref, o_ref, tm)


def _ffn(x, g, wg, wu, wd, g_final=None, *, tm=1024, tf=512):
    m, d = x.shape
    dff = wg.shape[1]
    nf = dff // tf
    final = g_final is not None
    in_specs = [pl.BlockSpec((tm, d), lambda i, f: (i, 0)),
                _resident((1, d), lambda i, f: (0, 0)),
                pl.BlockSpec((d, tf), lambda i, f: (0, f)),
                pl.BlockSpec((d, tf), lambda i, f: (0, f)),
                pl.BlockSpec((tf, d), lambda i, f: (f, 0))]
    args = [x, g.reshape(1, d), wg, wu, wd]
    if final:
        in_specs.append(_resident((1, d), lambda i, f: (0, 0)))
        args.append(g_final.reshape(1, d))
    return pl.pallas_call(
        functools.partial(_ffn_kernel, tm=tm, nf=nf, final=final),
        grid=(m // tm, nf),
        in_specs=in_specs,
        out_specs=pl.BlockSpec((tm, d), lambda i, f: (i, 0)),
        out_shape=jax.ShapeDtypeStruct((m, d), f32),
        scratch_shapes=[pltpu.VMEM((tm, d), bf16)],
        compiler_params=_cparams(("parallel", "arbitrary"), FFN_VMEM_LIMIT),
        name="ffn",
    )(*args)


def _t5_bucket(rel):
    nb = NUM_BUCKETS // 2
    max_exact = nb // 2
    ret = (rel > 0).astype(np.int32) * nb
    n = np.abs(rel)
    large = max_exact + (np.log(np.maximum(n, 1) / max_exact) / np.log(MAX_DISTANCE / max_exact)
                         * (nb - max_exact)).astype(np.int32)
    large = np.minimum(large, nb - 1)
    return (ret + np.where(n < max_exact, n, large)).astype(np.int32)


def _band_bias(table_cols, n_q, half, dilation):
    rel = np.arange(n_q + 2 * half)[None, :] - half - np.arange(n_q)[:, None]
    band = np.abs(rel) <= half
    vals = table_cols[_t5_bucket(rel * dilation)]
    vals = jnp.where(band[:, :, None], vals.astype(f32), NEG_INF)
    return vals.transpose(2, 0, 1)


def _a_geometry(d):
    run = A_CHUNK // d
    stride = run + 2 * A_HALF
    sub = min(run, A_SUB)
    n_keys = -(-(sub + 2 * A_HALF) // MXU_WIDTH) * MXU_WIDTH
    pad = (n_keys - sub - 2 * A_HALF) // 2
    return run, stride, sub, n_keys, pad


def _a_bias(table_cols, d):
    run, stride, sub, n_keys, pad = _a_geometry(d)
    band = _band_bias(table_cols, sub, A_HALF, d)
    band = jnp.pad(band, ((0, 0), (0, 0), (pad, pad)), constant_values=NEG_INF)
    return jnp.tile(band, (1, A_SUB // sub, 1))


def _attn_a_kernel(q_ref, kp_ref, kc_ref, kn_ref, vp_ref, vc_ref, vn_ref, bias_ref, o_ref,
                   kbuf, vbuf, *, d, seq_len):
    run, stride, sub, n_keys, pad = _a_geometry(d)
    n_sb = run // sub
    c = pl.program_id(1)
    for buf, p_ref, c_ref, n_ref in ((kbuf, kp_ref, kc_ref, kn_ref), (vbuf, vp_ref, vc_ref, vn_ref)):
        if pad:
            buf[0:pad] = jnp.zeros((pad, A_OUT), bf16)
            buf[pad + d * stride:] = jnp.zeros((pad, A_OUT), bf16)
        for r in range(d):
            base = pad + r * stride
            buf[base:base + A_HALF] = p_ref[(r + 1) * run - A_HALF:(r + 1) * run]
            buf[base + A_HALF:base + A_HALF + run] = c_ref[r * run:(r + 1) * run]
            buf[base + A_HALF + run:base + stride] = n_ref[r * run:r * run + A_HALF]
    pack = A_SUB // sub
    lane = lax.broadcasted_iota(jnp.int32, (A_SUB, HEAD_DIM), 1)
    row = lax.broadcasted_iota(jnp.int32, (A_SUB, 1), 0)
    own = [row >= p * sub for p in range(pack)]

    def pick(parts):
        out = parts[0]
        for p in range(1, pack):
            out = jnp.where(own[p], parts[p], out)
        return out

    for u in range(A_CHUNK // A_SUB):
        r, sb = divmod(u * pack, n_sb)
        q0 = u * A_SUB
        k0s = [(r + p) * stride + sb * sub for p in range(pack)]
        kpos = c * run + (sb * sub - A_HALF - pad) + lax.broadcasted_iota(jnp.int32, (1, n_keys), 1)
        valid = jnp.where(kpos >= 0, kpos, seq_len) < seq_len
        lse_tile = jnp.zeros((A_SUB, HEAD_DIM), f32)
        for h in range(A_HEADS_PER_GROUP):
            c0 = h * HEAD_DIM
            q = q_ref[q0:q0 + A_SUB, c0:c0 + HEAD_DIM]
            s = pick([lax.dot_general(q, kbuf[k0:k0 + n_keys, c0:c0 + HEAD_DIM], (((1,), (1,)), ((), ())),
                                      preferred_element_type=f32) for k0 in k0s])
            s = s * SCALE + bias_ref[h]
            s = jnp.where(valid, s, NEG_INF)
            m = jnp.max(s, axis=-1, keepdims=True)
            p = jnp.exp(s - m)
            den = jnp.sum(p, axis=-1, keepdims=True)
            pb = p.astype(bf16)
            o = pick([jnp.dot(pb, vbuf[k0:k0 + n_keys, c0:c0 + HEAD_DIM], preferred_element_type=f32)
                      for k0 in k0s]) / den
            o_ref[0, q0:q0 + A_SUB, c0:c0 + HEAD_DIM] = o
            lse_tile = jnp.where(lane == h, m + jnp.log(den), lse_tile)
        o_ref[0, q0:q0 + A_SUB, A_OUT:] = lse_tile


def _attn_a_group(proj, bias, g, d, bn, t):
    run, stride, sub, n_keys, pad = _a_geometry(d)
    nc = t // A_CHUNK
    pv = proj.reshape(N_AB_SLABS, bn, t, A_GROUP_IN)

    def spec(col, which):
        def idx(b, c):
            cc = (jnp.maximum(c - 1, 0), c, jnp.minimum(c + 1, nc - 1))[which]
            return (g, b, cc, col)
        return pl.BlockSpec((None, None, A_CHUNK, A_OUT), idx)

    return pl.pallas_call(
        functools.partial(_attn_a_kernel, d=d, seq_len=t // d),
        grid=(bn, nc),
        in_specs=[spec(0, 1), spec(1, 0), spec(1, 1), spec(1, 2), spec(2, 0), spec(2, 1), spec(2, 2),
                  _resident((A_HEADS_PER_GROUP, A_SUB, n_keys), lambda b, c: (0, 0, 0))],
        out_specs=pl.BlockSpec((1, A_CHUNK, A_OUT_W), lambda b, c: (b, c, 0)),
        out_shape=jax.ShapeDtypeStruct((bn, t, A_OUT_W), f32),
        scratch_shapes=[pltpu.VMEM((d * stride + 2 * pad, A_OUT), bf16),
                        pltpu.VMEM((d * stride + 2 * pad, A_OUT), bf16)],
        compiler_params=_cparams(("parallel", "arbitrary")),
        name=f"attn_a{g}",
    )(pv, pv, pv, pv, pv, pv, pv, bias)


def _attn_b_kernel(sink_ref, cur_ref, hp_ref, hn_ref, bias_ref, o_ref, kbuf, vbuf, *, tq, seq_len):
    i = pl.program_id(1)
    kw = B_KV_HEADS * HEAD_DIM
    k_off = B_OUT
    v_off = B_OUT + kw
    hw = B_HALF_WINDOW
    kbuf[0:hw] = hp_ref[:, 0:kw]
    kbuf[hw:hw + tq] = cur_ref[:, k_off:k_off + kw]
    kbuf[hw + tq:] = hn_ref[:, 0:kw]
    vbuf[0:hw] = hp_ref[:, kw:2 * kw]
    vbuf[hw:hw + tq] = cur_ref[:, v_off:v_off + kw]
    vbuf[hw + tq:] = hn_ref[:, kw:2 * kw]
    for sb in range(tq // B_SUB):
        r0 = sb * B_SUB
        kpos = i * tq + (r0 - hw) + lax.broadcasted_iota(jnp.int32, (1, B_KEYS), 1)
        valid = jnp.where(kpos >= 0, kpos, seq_len) < seq_len
        for g in range(B_KV_HEADS):
            qs = jnp.concatenate(
                [cur_ref[r0:r0 + B_SUB, (g * B_REP + hh) * HEAD_DIM:(g * B_REP + hh + 1) * HEAD_DIM]
                 for hh in range(B_REP)], axis=0)
            k = kbuf[r0:r0 + B_KEYS, g * HEAD_DIM:(g + 1) * HEAD_DIM]
            v = vbuf[r0:r0 + B_KEYS, g * HEAD_DIM:(g + 1) * HEAD_DIM]
            s_all = lax.dot_general(qs, k, (((1,), (1,)), ((), ())), preferred_element_type=f32)
            s_all = s_all * SCALE + bias_ref[g]
            s_all = jnp.where(valid, s_all, NEG_INF)
            ps, dens = [], []
            for hh in range(B_REP):
                s = s_all[hh * B_SUB:(hh + 1) * B_SUB]
                sk = sink_ref[g * B_REP + hh]
                m = jnp.maximum(jnp.max(s, axis=-1, keepdims=True), sk)
                p = jnp.exp(s - m)
                dens.append(jnp.sum(p, axis=-1, keepdims=True) + jnp.exp(sk - m))
                ps.append(p.astype(bf16))
            o_all = jnp.dot(jnp.concatenate(ps, axis=0), v, preferred_element_type=f32)
            for hh in range(B_REP):
                c0 = (g * B_REP + hh) * HEAD_DIM
                o = o_all[hh * B_SUB:(hh + 1) * B_SUB] / dens[hh]
                o_ref[0, r0:r0 + B_SUB, c0:c0 + HEAD_DIM] = o.astype(o_ref.dtype)


def _attn_b(proj, bias, sink, bn, t):
    tq = 256
    nblk = t // tq
    hb = tq // B_HALF_WINDOW
    n_halo = t // B_HALF_WINDOW
    kvw = 2 * B_KV_HEADS * HEAD_DIM
    slab = N_AB_SLABS - 1
    pv = proj.reshape(N_AB_SLABS, bn, t, B_IN)
    out = pl.pallas_call(
        functools.partial(_attn_b_kernel, tq=tq, seq_len=t),
        grid=(bn, nblk),
        in_specs=[pl.BlockSpec(memory_space=pltpu.SMEM),
                  pl.BlockSpec((None, None, tq, B_IN), lambda b, i: (slab, b, i, 0)),
                  pl.BlockSpec((None, None, B_HALF_WINDOW, kvw),
                               lambda b, i: (slab, b, jnp.maximum(i * hb - 1, 0), B_IN // kvw - 1)),
                  pl.BlockSpec((None, None, B_HALF_WINDOW, kvw),
                               lambda b, i: (slab, b, jnp.minimum((i + 1) * hb, n_halo - 1), B_IN // kvw - 1)),
                  _resident((B_KV_HEADS, B_REP * B_SUB, B_KEYS), lambda b, i: (0, 0, 0))],
        out_specs=pl.BlockSpec((1, tq, B_OUT), lambda b, i: (b, i, 0)),
        out_shape=jax.ShapeDtypeStruct((bn, t, B_OUT), bf16),
        scratch_shapes=[pltpu.VMEM((tq + 2 * B_HALF_WINDOW, B_KV_HEADS * HEAD_DIM), bf16),
                        pltpu.VMEM((tq + 2 * B_HALF_WINDOW, B_KV_HEADS * HEAD_DIM), bf16)],
        compiler_params=_cparams(("parallel", "arbitrary")),
        name="attn_b",
    )(sink, pv, pv, pv, bias)
    return out.reshape(bn * t, B_OUT)


def _out_ab_kernel(x_ref, a0_ref, a1_ref, a2_ref, ob_ref, w_ref, o_ref, ilv_ref, mix_ref, *, tm):
    n_slabs = A_OUT_W // HEAD_DIM
    for gi, (a_ref, (_, d)) in enumerate(((a1_ref, DILATIONS[1]), (a2_ref, DILATIONS[2]))):
        per = tm // d
        for r in range(d):
            for s in range(n_slabs):
                ilv_ref[gi, s, pl.ds(r, per, stride=d), :] = a_ref[0, r, :, s * HEAD_DIM:(s + 1) * HEAD_DIM]
    for h in range(A_HEADS_PER_GROUP):
        c0 = h * HEAD_DIM
        outs = [a0_ref[:, c0:c0 + HEAD_DIM], ilv_ref[0, h], ilv_ref[1, h]]
        lses = [a0_ref[:, A_OUT + h:A_OUT + h + 1],
                ilv_ref[0, n_slabs - 1, :, h:h + 1],
                ilv_ref[1, n_slabs - 1, :, h:h + 1]]
        mx = jnp.maximum(jnp.maximum(lses[0], lses[1]), lses[2])
        es = [jnp.exp(l - mx) for l in lses]
        tot = es[0] + es[1] + es[2]
        acc = (es[0] / tot) * outs[0]
        acc += (es[1] / tot) * outs[1]
        acc += (es[2] / tot) * outs[2]
        mix_ref[:, c0:c0 + HEAD_DIM] = acc.astype(bf16)
    mix_ref[:, A_OUT:] = ob_ref[...]
    o_ref[...] = x_ref[...] + jnp.dot(mix_ref[...], w_ref[...], preferred_element_type=f32)


def _out_ab(x, a_outs, o_b, w, *, tm=512):
    m, d = x.shape
    row = lambda i: (i, 0)
    tiles = A_CHUNK // tm
    a0 = a_outs[0].reshape(m, A_OUT_W)
    dil = [dd for _, dd in DILATIONS]
    a1 = a_outs[1].reshape(m // A_CHUNK, dil[1], A_CHUNK // dil[1], A_OUT_W)
    a2 = a_outs[2].reshape(m // A_CHUNK, dil[2], A_CHUNK // dil[2], A_OUT_W)

    def deint(dd):
        return pl.BlockSpec((1, dd, tm // dd, A_OUT_W), lambda i: (i // tiles, 0, i % tiles, 0))

    return pl.pallas_call(
        functools.partial(_out_ab_kernel, tm=tm),
        grid=(m // tm,),
        in_specs=[pl.BlockSpec((tm, d), row),
                  pl.BlockSpec((tm, A_OUT_W), row),
                  deint(dil[1]),
                  deint(dil[2]),
                  pl.BlockSpec((tm, B_OUT), row),
                  _resident((AB_OUT, d), lambda i: (0, 0))],
        out_specs=pl.BlockSpec((tm, d), row),
        out_shape=jax.ShapeDtypeStruct((m, d), f32),
        scratch_shapes=[pltpu.VMEM((2, A_OUT_W // HEAD_DIM, tm, HEAD_DIM), f32),
                        pltpu.VMEM((tm, AB_OUT), bf16)],
        compiler_params=_cparams(("parallel",)),
        name="out_ab",
    )(x, a0, a1, a2, o_b, w)


def _na_bias(rpb):
    n_heads = rpb.shape[0]
    n_krows = C_KBLKS * C_QROWS
    c = np.arange(GRID_W)
    cs = np.clip(c - NA_COLS // 2, 0, GRID_W - NA_COLS)
    col_ok = (c[None, :] >= cs[:, None]) & (c[None, :] < cs[:, None] + NA_COLS)
    dc = np.clip(c[None, :] - c[:, None] + NA_COLS - 1, 0, 2 * NA_COLS - 2)
    onehot = (dc[None] == np.arange(2 * NA_COLS - 1)[:, None, None]).astype(np.float32)
    tiles = jnp.einsum('hdc,cqk->hqdk', rpb.astype(f32), onehot, precision=lax.Precision.HIGHEST)
    tiles = jnp.where(col_ok[None, :, None, :], tiles, NEG_INF)
    out = []
    for case in range(3):
        rows = []
        for qi in range(C_QROWS):
            first = (0, qi, C_QROWS)[case]
            dr0 = first - qi - case * C_QROWS + NA_ROWS - 1
            band = tiles[:, :, dr0:dr0 + NA_ROWS, :]
            rows.append(jnp.pad(band, ((0, 0), (0, 0), (first, n_krows - NA_ROWS - first), (0, 0)),
                                constant_values=NEG_INF))
        out.append(jnp.stack(rows, axis=1).reshape(n_heads, C_QTOK, n_krows * GRID_W))
    return jnp.stack(out)


def _attn_c_kernel(q_ref, k0_ref, k1_ref, k2_ref, v0_ref, v1_ref, v2_ref, bias_ref, o_ref):
    k_refs = (k0_ref, k1_ref, k2_ref)
    v_refs = (v0_ref, v1_ref, v2_ref)
    for h in range(C_HG):
        c0 = h * HEAD_DIM
        q = q_ref[0, :, c0:c0 + HEAD_DIM]
        ss = []
        for j in range(C_KBLKS):
            s = lax.dot_general(q, k_refs[j][0, :, c0:c0 + HEAD_DIM], (((1,), (1,)), ((), ())),
                                preferred_element_type=f32)
            ss.append(s * SCALE + bias_ref[0, h, :, j * C_QTOK:(j + 1) * C_QTOK])
        m = jnp.max(ss[0], axis=-1, keepdims=True)
        for j in range(1, C_KBLKS):
            m = jnp.maximum(m, jnp.max(ss[j], axis=-1, keepdims=True))
        den = None
        o = None
        for j in range(C_KBLKS):
            p = jnp.exp(ss[j] - m)
            dj = jnp.sum(p, axis=-1, keepdims=True)
            oj = jnp.dot(p.astype(bf16), v_refs[j][0, :, c0:c0 + HEAD_DIM], preferred_element_type=f32)
            den = dj if den is None else den + dj
            o = oj if o is None else o + oj
        o_ref[0, :, c0:c0 + HEAD_DIM] = (o / den).astype(o_ref.dtype)


def _attn_c(qkv, bias):
    bn, t, _ = qkv.shape
    nblk = t // C_QTOK
    hgw = C_HG * HEAD_DIM
    n_hg = C_HEADS // C_HG
    k_cb = C_OUT // hgw
    v_cb = 2 * C_OUT // hgw

    def kv_spec(cb, j):
        return pl.BlockSpec((1, C_QTOK, hgw),
                            lambda g, b, i: (b, jnp.clip(i - 1, 0, nblk - C_KBLKS) + j, cb + g))

    def case(i):
        return jnp.where(i == 0, 0, jnp.where(i == nblk - 1, 2, 1))

    out = pl.pallas_call(
        _attn_c_kernel,
        grid=(n_hg, bn, nblk),
        in_specs=[pl.BlockSpec((1, C_QTOK, hgw), lambda g, b, i: (b, i, g)),
                  kv_spec(k_cb, 0), kv_spec(k_cb, 1), kv_spec(k_cb, 2),
                  kv_spec(v_cb, 0), kv_spec(v_cb, 1), kv_spec(v_cb, 2),
                  pl.BlockSpec((1, C_HG, C_QTOK, C_KBLKS * C_QTOK), lambda g, b, i: (case(i), g, 0, 0))],
        out_specs=pl.BlockSpec((1, C_QTOK, hgw), lambda g, b, i: (b, i, g)),
        out_shape=jax.ShapeDtypeStruct((bn, t, C_OUT), bf16),
        compiler_params=_cparams(("parallel", "parallel", "arbitrary")),
        name="attn_c",
    )(qkv, qkv, qkv, qkv, qkv, qkv, qkv, bias)
    return out.reshape(bn * t, C_OUT)


def _out_proj_kernel(x_ref, a_ref, w_ref, o_ref):
    o_ref[...] = x_ref[...] + jnp.dot(a_ref[...], w_ref[...], preferred_element_type=f32)


def _out_proj(x, a, w, *, tm=512):
    m, d = x.shape
    k = a.shape[1]
    row = lambda i: (i, 0)
    return pl.pallas_call(
        _out_proj_kernel,
        grid=(m // tm,),
        in_specs=[pl.BlockSpec((tm, d), row),
                  pl.BlockSpec((tm, k), row),
                  _resident((k, d), lambda i: (0, 0))],
        out_specs=pl.BlockSpec((tm, d), row),
        out_shape=jax.ShapeDtypeStruct((m, d), f32),
        compiler_params=_cparams(("parallel",)),
        name="out_c",
    )(x, a, w)


def _mixer_ab(x, bn, t, g_norm, w_in, w_out, sink, a_biases, b_bias, perms):
    proj = _norm_proj_ab(x, g_norm, w_in, perms)
    a_outs = [_attn_a_group(proj, a_biases[g], g, d, bn, t) for g, (_, d) in enumerate(DILATIONS)]
    o_b = _attn_b(proj, b_bias, sink, bn, t)
    return _out_ab(x, a_outs, o_b, w_out)


def _mixer_c(x, bn, t, g_norm, w_in, w_out, c_bias):
    qkv = _norm_matmul(x, g_norm, w_in).reshape(bn, t, C_IN)
    return _out_proj(x, _attn_c(qkv, c_bias), w_out)


def _trunk(x3, w_in_ab, w_out_ab, sink_b, w_in_c, w_out_c, a_biases, b_bias, c_biases, perms,
           norm_mix, norm_ffn, w_gate, w_up, w_down, norm_final):
    bn, t, d = x3.shape
    x = x3.reshape(bn * t, d)
    for layer in range(DEPTH):
        j = layer // 2
        if layer % 2 == 0:
            x = _mixer_ab(x, bn, t, norm_mix[layer], w_in_ab[j], w_out_ab[j], sink_b[j], a_biases, b_bias, perms)
        else:
            x = _mixer_c(x, bn, t, norm_mix[layer], w_in_c[j], w_out_c[j], c_biases[j])
        x = _ffn(x, norm_ffn[layer], w_gate[layer], w_up[layer], w_down[layer],
                 norm_final if layer == DEPTH - 1 else None)
    return x.reshape(bn, t, d)


def _tables(t5_table, rpb_c):
    a_biases = []
    for g, (_, d) in enumerate(DILATIONS):
        cols = t5_table[:, g * A_HEADS_PER_GROUP:(g + 1) * A_HEADS_PER_GROUP]
        a_biases.append(_a_bias(cols, d))
    b_bias = _band_bias(t5_table[:, A_HEADS:], B_SUB, B_HALF_WINDOW, 1)
    b_bias = b_bias.reshape(B_KV_HEADS, B_REP * B_SUB, B_KEYS)
    c_biases = [_na_bias(rpb_c[j]) for j in range(rpb_c.shape[0])]
    perms = jnp.asarray(np.stack([_deinterleave_perm(d) for _, d in DILATIONS[1:]]), bf16)
    return a_biases, b_bias, c_biases, perms


def kernel(x_prompt, x_sample, w_in_ab, w_out_ab, sink_b, w_in_c, w_out_c, rpb_c, t5_table, norm_mix, norm_ffn, w_gate, w_up, w_down, norm_final):
    a_biases, b_bias, c_biases, perms = _tables(t5_table, rpb_c)
    weights = (w_in_ab.astype(bf16), w_out_ab.astype(bf16), sink_b, w_in_c.astype(bf16), w_out_c.astype(bf16),
               a_biases, b_bias, c_biases, perms, norm_mix, norm_ffn,
               w_gate.astype(bf16), w_up.astype(bf16), w_down.astype(bf16), norm_final)
    return (_trunk(x_prompt, *weights), _trunk(x_sample, *weights))
```

```python
import functools

import numpy as np
import jax
import jax.numpy as jnp
from jax import lax
from jax.experimental import pallas as pl
from jax.experimental.pallas import tpu as pltpu

D_MODEL = 2048
DEPTH = 4
HEAD_DIM = 128
DILATIONS = ((128, 1), (512, 4), (2048, 16))
A_HEADS_PER_GROUP = 4
A_HEADS = A_HEADS_PER_GROUP * len(DILATIONS)
B_Q_HEADS = 8
B_KV_HEADS = 2
B_REP = B_Q_HEADS // B_KV_HEADS
B_HALF_WINDOW = 128
C_HEADS = 16
GRID_W = 64
NA_ROWS = 8
NA_COLS = 16
NUM_BUCKETS = 32
MAX_DISTANCE = 1024
RMS_EPS = 1e-6
A_GROUP_IN = 3 * A_HEADS_PER_GROUP * HEAD_DIM
A_IN = len(DILATIONS) * A_GROUP_IN
B_IN = (B_Q_HEADS + 2 * B_KV_HEADS) * HEAD_DIM
AB_IN = A_IN + B_IN
N_AB_SLABS = AB_IN // A_GROUP_IN
A_OUT = A_HEADS_PER_GROUP * HEAD_DIM
A_OUT_W = A_OUT + HEAD_DIM
B_OUT = B_Q_HEADS * HEAD_DIM
AB_OUT = A_OUT + B_OUT
C_IN = 3 * C_HEADS * HEAD_DIM
C_OUT = C_HEADS * HEAD_DIM
SCALE = HEAD_DIM ** -0.5
NEG_INF = -1e30

A_HALF = 64
A_CHUNK = 1024
A_SUB = 128
MXU_WIDTH = 256
PERM_PIECE = 256
B_SUB = 128
B_KEYS = B_SUB + 2 * B_HALF_WINDOW
C_QROWS = 4
C_QTOK = C_QROWS * GRID_W
C_KBLKS = 3
C_HG = 4

VMEM_LIMIT = 52 * 1024 * 1024
FFN_VMEM_LIMIT = 58 * 1024 * 1024

f32 = jnp.float32
bf16 = jnp.bfloat16


def _cparams(sem, vmem_limit=VMEM_LIMIT):
    return pltpu.CompilerParams(dimension_semantics=sem, vmem_limit_bytes=vmem_limit)


def _resident(shape, index_map):
    return pl.BlockSpec(shape, index_map, pipeline_mode=pl.Buffered(1))


def _rms_rows(x_ref, g_ref, dst_ref, rows, chunk=64):
    def body(c, carry):
        r = pl.multiple_of(c * chunk, chunk)
        x = x_ref[pl.ds(r, chunk), :]
        ms = jnp.mean(x * x, axis=-1, keepdims=True)
        dst_ref[pl.ds(r, chunk), :] = ((x * lax.rsqrt(ms + RMS_EPS)) * g_ref[...]).astype(dst_ref.dtype)
        return carry
    lax.fori_loop(0, rows // chunk, body, 0, unroll=2)


def _norm_matmul_kernel(x_ref, g_ref, w_ref, o_ref, h_ref, *, tm):
    @pl.when(pl.program_id(1) == 0)
    def _():
        _rms_rows(x_ref, g_ref, h_ref, tm)
    o_ref[...] = jnp.dot(h_ref[...], w_ref[...], preferred_element_type=f32).astype(o_ref.dtype)


def _norm_matmul(x, g, w, li, *, tm=1024, tn=1536):
    m, d = x.shape
    n = w.shape[2]
    return pl.pallas_call(
        functools.partial(_norm_matmul_kernel, tm=tm),
        grid=(m // tm, n // tn),
        in_specs=[pl.BlockSpec((tm, d), lambda i, j: (i, 0)),
                  _resident((1, d), lambda i, j: (0, 0)),
                  pl.BlockSpec((None, d, tn), lambda i, j: (li, 0, j))],
        out_specs=pl.BlockSpec((tm, tn), lambda i, j: (i, j)),
        out_shape=jax.ShapeDtypeStruct((m, n), bf16),
        scratch_shapes=[pltpu.VMEM((tm, d), bf16)],
        compiler_params=_cparams(("parallel", "arbitrary")),
        name="norm_proj",
    )(x, g.reshape(1, d), w)


def _deinterleave_perm(d):
    per = PERM_PIECE // d
    p = np.zeros((PERM_PIECE, PERM_PIECE), np.float32)
    for r in range(d):
        for v in range(per):
            p[r * per + v, v * d + r] = 1.0
    return p


def _norm_proj_ab_kernel(x_ref, g_ref, w_ref, perm_ref, o_ref, h_ref, *, tm, steps_per_slab):
    j = pl.program_id(1)

    @pl.when(j == 0)
    def _():
        _rms_rows(x_ref, g_ref, h_ref, tm)

    t = jnp.dot(h_ref[...], w_ref[...], preferred_element_type=f32).astype(bf16)
    slab = j // steps_per_slab

    @pl.when(jnp.logical_or(slab == 0, slab == N_AB_SLABS - 1))
    def _():
        o_ref[0] = t

    for g, (_, d) in enumerate(DILATIONS):
        if d == 1:
            continue

        @pl.when(slab == g)
        def _(g=g, d=d):
            run = tm // d
            per = PERM_PIECE // d
            for p in range(tm // PERM_PIECE):
                piece = jnp.dot(perm_ref[g - 1], t[p * PERM_PIECE:(p + 1) * PERM_PIECE, :],
                                preferred_element_type=f32).astype(bf16)
                for r in range(d):
                    o_ref[0, r * run + p * per:r * run + (p + 1) * per, :] = piece[r * per:(r + 1) * per, :]


def _norm_proj_ab(x, g, w, li, perms, *, tn=1536):
    m, d = x.shape
    tm = A_CHUNK
    steps_per_slab = A_GROUP_IN // tn
    return pl.pallas_call(
        functools.partial(_norm_proj_ab_kernel, tm=tm, steps_per_slab=steps_per_slab),
        grid=(m // tm, AB_IN // tn),
        in_specs=[pl.BlockSpec((tm, d), lambda i, j: (i, 0)),
                  _resident((1, d), lambda i, j: (0, 0)),
                  pl.BlockSpec((None, d, tn), lambda i, j: (li, 0, j)),
                  _resident(perms.shape, lambda i, j: (0, 0, 0))],
        out_specs=pl.BlockSpec((1, tm, tn), lambda i, j: (j // steps_per_slab, i, j % steps_per_slab)),
        out_shape=jax.ShapeDtypeStruct((N_AB_SLABS, m, A_GROUP_IN), bf16),
        scratch_shapes=[pltpu.VMEM((tm, d), bf16)],
        compiler_params=_cparams(("parallel", "arbitrary")),
        name="norm_proj_ab",
    )(x, g.reshape(1, d), w, perms)


def _ffn_kernel(x_ref, g_ref, wg_ref, wu_ref, wd_ref, *rest, tm, nf, final):
    if final:
        gf_ref, o_ref, h_ref = rest
    else:
        o_ref, h_ref = rest
    f = pl.program_id(1)

    @pl.when(f == 0)
    def _():
        _rms_rows(x_ref, g_ref, h_ref, tm)
        o_ref[...] = x_ref[...]

    h = h_ref[...]
    gate = jnp.dot(h, wg_ref[...], preferred_element_type=f32)
    up = jnp.dot(h, wu_ref[...], preferred_element_type=f32)
    act = (gate * jax.nn.sigmoid(gate) * up).astype(bf16)
    o_ref[...] += jnp.dot(act, wd_ref[...], preferred_element_type=f32)

    if final:
        @pl.when(f == nf - 1)
        def _():
            _rms_rows(o_ref, gf_ref, o_ref, tm)


def _ffn(x, g, wg, wu, wd, li, g_final=None, *, tm=1024, tf=512):
    m, d = x.shape
    dff = wg.shape[2]
    nf = dff // tf
    final = g_final is not None
    in_specs = [pl.BlockSpec((tm, d), lambda i, f: (i, 0)),
                _resident((1, d), lambda i, f: (0, 0)),
                pl.BlockSpec((None, d, tf), lambda i, f: (li, 0, f)),
                pl.BlockSpec((None, d, tf), lambda i, f: (li, 0, f)),
                pl.BlockSpec((None, tf, d), lambda i, f: (li, f, 0))]
    args = [x, g.reshape(1, d), wg, wu, wd]
    if final:
        in_specs.append(_resident((1, d), lambda i, f: (0, 0)))
        args.append(g_final.reshape(1, d))
    return pl.pallas_call(
        functools.partial(_ffn_kernel, tm=tm, nf=nf, final=final),
        grid=(m // tm, nf),
        in_specs=in_specs,
        out_specs=pl.BlockSpec((tm, d), lambda i, f: (i, 0)),
        out_shape=jax.ShapeDtypeStruct((m, d), f32),
        scratch_shapes=[pltpu.VMEM((tm, d), bf16)],
        compiler_params=_cparams(("parallel", "arbitrary"), FFN_VMEM_LIMIT),
        name="ffn",
    )(*args)


def _t5_bucket(rel):
    nb = NUM_BUCKETS // 2
    max_exact = nb // 2
    ret = (rel > 0).astype(np.int32) * nb
    n = np.abs(rel)
    large = max_exact + (np.log(np.maximum(n, 1) / max_exact) / np.log(MAX_DISTANCE / max_exact)
                         * (nb - max_exact)).astype(np.int32)
    large = np.minimum(large, nb - 1)
    return (ret + np.where(n < max_exact, n, large)).astype(np.int32)


def _band_bias(table_cols, n_q, half, dilation):
    n_k = n_q + 2 * half
    w = n_q + n_k
    rel = np.arange(w) - (n_q - 1) - half
    diag = table_cols[_t5_bucket(rel * dilation)].astype(f32)
    diag = jnp.where((np.abs(rel) <= half)[:, None], diag, NEG_INF).T
    flat = jnp.tile(diag, (1, n_q + 1))[:, n_q - 1:n_q - 1 + n_q * (w - 1)]
    return flat.reshape(-1, n_q, w - 1)[:, :, :n_k]


def _a_geometry(d):
    run = A_CHUNK // d
    stride = run + 2 * A_HALF
    sub = min(run, A_SUB)
    n_keys = -(-(sub + 2 * A_HALF) // MXU_WIDTH) * MXU_WIDTH
    pad = (n_keys - sub - 2 * A_HALF) // 2
    return run, stride, sub, n_keys, pad


def _a_bias(table_cols, d):
    run, stride, sub, n_keys, pad = _a_geometry(d)
    band = _band_bias(table_cols, sub, A_HALF, d)
    band = jnp.pad(band, ((0, 0), (0, 0), (pad, pad)), constant_values=NEG_INF)
    return jnp.tile(band, (1, A_SUB // sub, 1))


def _attn_a_kernel(q_ref, kp_ref, kc_ref, kn_ref, vp_ref, vc_ref, vn_ref, bias_ref, o_ref,
                   kbuf, vbuf, *, d, seq_len):
    run, stride, sub, n_keys, pad = _a_geometry(d)
    n_sb = run // sub
    c = pl.program_id(1)
    for buf, p_ref, c_ref, n_ref in ((kbuf, kp_ref, kc_ref, kn_ref), (vbuf, vp_ref, vc_ref, vn_ref)):
        if pad:
            buf[0:pad] = jnp.zeros((pad, A_OUT), bf16)
            buf[pad + d * stride:] = jnp.zeros((pad, A_OUT), bf16)
        for r in range(d):
            base = pad + r * stride
            buf[base:base + A_HALF] = p_ref[(r + 1) * run - A_HALF:(r + 1) * run]
            buf[base + A_HALF:base + A_HALF + run] = c_ref[r * run:(r + 1) * run]
            buf[base + A_HALF + run:base + stride] = n_ref[r * run:r * run + A_HALF]
    pack = A_SUB // sub
    lane = lax.broadcasted_iota(jnp.int32, (A_SUB, HEAD_DIM), 1)
    row = lax.broadcasted_iota(jnp.int32, (A_SUB, 1), 0)
    own = [row >= p * sub for p in range(pack)]

    def pick(parts):
        out = parts[0]
        for p in range(1, pack):
            out = jnp.where(own[p], parts[p], out)
        return out

    for u in range(A_CHUNK // A_SUB):
        r, sb = divmod(u * pack, n_sb)
        q0 = u * A_SUB
        k0s = [(r + p) * stride + sb * sub for p in range(pack)]
        kpos = c * run + (sb * sub - A_HALF - pad) + lax.broadcasted_iota(jnp.int32, (1, n_keys), 1)
        valid = jnp.where(kpos >= 0, kpos, seq_len) < seq_len
        lse_tile = jnp.zeros((A_SUB, HEAD_DIM), f32)
        for h in range(A_HEADS_PER_GROUP):
            c0 = h * HEAD_DIM
            q = q_ref[q0:q0 + A_SUB, c0:c0 + HEAD_DIM]
            s = pick([lax.dot_general(q, kbuf[k0:k0 + n_keys, c0:c0 + HEAD_DIM], (((1,), (1,)), ((), ())),
                                      preferred_element_type=f32) for k0 in k0s])
            s = s * SCALE + bias_ref[h]
            s = jnp.where(valid, s, NEG_INF)
            m = jnp.max(s, axis=-1, keepdims=True)
            p = jnp.exp(s - m)
            den = jnp.sum(p, axis=-1, keepdims=True)
            pb = p.astype(bf16)
            o = pick([jnp.dot(pb, vbuf[k0:k0 + n_keys, c0:c0 + HEAD_DIM], preferred_element_type=f32)
                      for k0 in k0s]) / den
            o_ref[0, q0:q0 + A_SUB, c0:c0 + HEAD_DIM] = o
            lse_tile = jnp.where(lane == h, m + jnp.log(den), lse_tile)
        o_ref[0, q0:q0 + A_SUB, A_OUT:] = lse_tile


def _attn_a_group(proj, bias, g, d, bn, t):
    run, stride, sub, n_keys, pad = _a_geometry(d)
    nc = t // A_CHUNK
    pv = proj.reshape(N_AB_SLABS, bn, t, A_GROUP_IN)

    def spec(col, which):
        def idx(b, c):
            cc = (jnp.maximum(c - 1, 0), c, jnp.minimum(c + 1, nc - 1))[which]
            return (g, b, cc, col)
        return pl.BlockSpec((None, None, A_CHUNK, A_OUT), idx)

    return pl.pallas_call(
        functools.partial(_attn_a_kernel, d=d, seq_len=t // d),
        grid=(bn, nc),
        in_specs=[spec(0, 1), spec(1, 0), spec(1, 1), spec(1, 2), spec(2, 0), spec(2, 1), spec(2, 2),
                  _resident((A_HEADS_PER_GROUP, A_SUB, n_keys), lambda b, c: (0, 0, 0))],
        out_specs=pl.BlockSpec((1, A_CHUNK, A_OUT_W), lambda b, c: (b, c, 0)),
        out_shape=jax.ShapeDtypeStruct((bn, t, A_OUT_W), f32),
        scratch_shapes=[pltpu.VMEM((d * stride + 2 * pad, A_OUT), bf16),
                        pltpu.VMEM((d * stride + 2 * pad, A_OUT), bf16)],
        compiler_params=_cparams(("parallel", "arbitrary")),
        name=f"attn_a{g}",
    )(pv, pv, pv, pv, pv, pv, pv, bias)


def _attn_b_kernel(sink_ref, cur_ref, hp_ref, hn_ref, bias_ref, o_ref, kbuf, vbuf, *, tq, seq_len):
    i = pl.program_id(1)
    kw = B_KV_HEADS * HEAD_DIM
    k_off = B_OUT
    v_off = B_OUT + kw
    hw = B_HALF_WINDOW
    kbuf[0:hw] = hp_ref[:, 0:kw]
    kbuf[hw:hw + tq] = cur_ref[:, k_off:k_off + kw]
    kbuf[hw + tq:] = hn_ref[:, 0:kw]
    vbuf[0:hw] = hp_ref[:, kw:2 * kw]
    vbuf[hw:hw + tq] = cur_ref[:, v_off:v_off + kw]
    vbuf[hw + tq:] = hn_ref[:, kw:2 * kw]
    for sb in range(tq // B_SUB):
        r0 = sb * B_SUB
        kpos = i * tq + (r0 - hw) + lax.broadcasted_iota(jnp.int32, (1, B_KEYS), 1)
        valid = jnp.where(kpos >= 0, kpos, seq_len) < seq_len
        for g in range(B_KV_HEADS):
            qs = jnp.concatenate(
                [cur_ref[r0:r0 + B_SUB, (g * B_REP + hh) * HEAD_DIM:(g * B_REP + hh + 1) * HEAD_DIM]
                 for hh in range(B_REP)], axis=0)
            k = kbuf[r0:r0 + B_KEYS, g * HEAD_DIM:(g + 1) * HEAD_DIM]
            v = vbuf[r0:r0 + B_KEYS, g * HEAD_DIM:(g + 1) * HEAD_DIM]
            s_all = lax.dot_general(qs, k, (((1,), (1,)), ((), ())), preferred_element_type=f32)
            s_all = s_all * SCALE + bias_ref[g]
            s_all = jnp.where(valid, s_all, NEG_INF)
            ps, dens = [], []
            for hh in range(B_REP):
                s = s_all[hh * B_SUB:(hh + 1) * B_SUB]
                sk = sink_ref[g * B_REP + hh]
                m = jnp.maximum(jnp.max(s, axis=-1, keepdims=True), sk)
                p = jnp.exp(s - m)
                dens.append(jnp.sum(p, axis=-1, keepdims=True) + jnp.exp(sk - m))
                ps.append(p.astype(bf16))
            o_all = jnp.dot(jnp.concatenate(ps, axis=0), v, preferred_element_type=f32)
            for hh in range(B_REP):
                c0 = (g * B_REP + hh) * HEAD_DIM
                o = o_all[hh * B_SUB:(hh + 1) * B_SUB] / dens[hh]
                o_ref[0, r0:r0 + B_SUB, c0:c0 + HEAD_DIM] = o.astype(o_ref.dtype)


def _attn_b(proj, bias, sink, bn, t):
    tq = 256
    nblk = t // tq
    hb = tq // B_HALF_WINDOW
    n_halo = t // B_HALF_WINDOW
    kvw = 2 * B_KV_HEADS * HEAD_DIM
    slab = N_AB_SLABS - 1
    pv = proj.reshape(N_AB_SLABS, bn, t, B_IN)
    out = pl.pallas_call(
        functools.partial(_attn_b_kernel, tq=tq, seq_len=t),
        grid=(bn, nblk),
        in_specs=[pl.BlockSpec(memory_space=pltpu.SMEM),
                  pl.BlockSpec((None, None, tq, B_IN), lambda b, i: (slab, b, i, 0)),
                  pl.BlockSpec((None, None, B_HALF_WINDOW, kvw),
                               lambda b, i: (slab, b, jnp.maximum(i * hb - 1, 0), B_IN // kvw - 1)),
                  pl.BlockSpec((None, None, B_HALF_WINDOW, kvw),
                               lambda b, i: (slab, b, jnp.minimum((i + 1) * hb, n_halo - 1), B_IN // kvw - 1)),
                  _resident((B_KV_HEADS, B_REP * B_SUB, B_KEYS), lambda b, i: (0, 0, 0))],
        out_specs=pl.BlockSpec((1, tq, B_OUT), lambda b, i: (b, i, 0)),
        out_shape=jax.ShapeDtypeStruct((bn, t, B_OUT), bf16),
        scratch_shapes=[pltpu.VMEM((tq + 2 * B_HALF_WINDOW, B_KV_HEADS * HEAD_DIM), bf16),
                        pltpu.VMEM((tq + 2 * B_HALF_WINDOW, B_KV_HEADS * HEAD_DIM), bf16)],
        compiler_params=_cparams(("parallel", "arbitrary")),
        name="attn_b",
    )(sink, pv, pv, pv, bias)
    return out.reshape(bn * t, B_OUT)


def _out_ab_kernel(x_ref, a0_ref, a1_ref, a2_ref, ob_ref, w_ref, o_ref, ilv_ref, mix_ref, *, tm):
    n_slabs = A_OUT_W // HEAD_DIM
    for gi, (a_ref, (_, d)) in enumerate(((a1_ref, DILATIONS[1]), (a2_ref, DILATIONS[2]))):
        per = tm // d
        for r in range(d):
            for s in range(n_slabs):
                ilv_ref[gi, s, pl.ds(r, per, stride=d), :] = a_ref[0, r, :, s * HEAD_DIM:(s + 1) * HEAD_DIM]
    for h in range(A_HEADS_PER_GROUP):
        c0 = h * HEAD_DIM
        outs = [a0_ref[:, c0:c0 + HEAD_DIM], ilv_ref[0, h], ilv_ref[1, h]]
        lses = [a0_ref[:, A_OUT + h:A_OUT + h + 1],
                ilv_ref[0, n_slabs - 1, :, h:h + 1],
                ilv_ref[1, n_slabs - 1, :, h:h + 1]]
        mx = jnp.maximum(jnp.maximum(lses[0], lses[1]), lses[2])
        es = [jnp.exp(l - mx) for l in lses]
        tot = es[0] + es[1] + es[2]
        acc = (es[0] / tot) * outs[0]
        acc += (es[1] / tot) * outs[1]
        acc += (es[2] / tot) * outs[2]
        mix_ref[:, c0:c0 + HEAD_DIM] = acc.astype(bf16)
    y = jnp.dot(ob_ref[...], w_ref[A_OUT:, :], preferred_element_type=f32)
    y += jnp.dot(mix_ref[...], w_ref[:A_OUT, :], preferred_element_type=f32)
    o_ref[...] = x_ref[...] + y


def _out_ab(x, a_outs, o_b, w, li, *, tm=512):
    m, d = x.shape
    row = lambda i: (i, 0)
    tiles = A_CHUNK // tm
    a0 = a_outs[0].reshape(m, A_OUT_W)
    dil = [dd for _, dd in DILATIONS]
    a1 = a_outs[1].reshape(m // A_CHUNK, dil[1], A_CHUNK // dil[1], A_OUT_W)
    a2 = a_outs[2].reshape(m // A_CHUNK, dil[2], A_CHUNK // dil[2], A_OUT_W)

    def deint(dd):
        return pl.BlockSpec((1, dd, tm // dd, A_OUT_W), lambda i: (i // tiles, 0, i % tiles, 0))

    return pl.pallas_call(
        functools.partial(_out_ab_kernel, tm=tm),
        grid=(m // tm,),
        in_specs=[pl.BlockSpec((tm, d), row),
                  pl.BlockSpec((tm, A_OUT_W), row),
                  deint(dil[1]),
                  deint(dil[2]),
                  pl.BlockSpec((tm, B_OUT), row),
                  _resident((None, AB_OUT, d), lambda i: (li, 0, 0))],
        out_specs=pl.BlockSpec((tm, d), row),
        out_shape=jax.ShapeDtypeStruct((m, d), f32),
        scratch_shapes=[pltpu.VMEM((2, A_OUT_W // HEAD_DIM, tm, HEAD_DIM), f32),
                        pltpu.VMEM((tm, A_OUT), bf16)],
        compiler_params=_cparams(("parallel",)),
        name="out_ab",
    )(x, a0, a1, a2, o_b, w)


def _na_bias(rpb):
    n_heads = rpb.shape[0]
    n_krows = C_KBLKS * C_QROWS
    c = np.arange(GRID_W)
    cs = np.clip(c - NA_COLS // 2, 0, GRID_W - NA_COLS)
    col_ok = (c[None, :] >= cs[:, None]) & (c[None, :] < cs[:, None] + NA_COLS)
    dc = np.clip(c[None, :] - c[:, None] + NA_COLS - 1, 0, 2 * NA_COLS - 2)
    onehot = (dc[None] == np.arange(2 * NA_COLS - 1)[:, None, None]).astype(np.float32)
    tiles = jnp.einsum('hdc,cqk->hqdk', rpb.astype(f32), onehot, precision=lax.Precision.HIGHEST)
    tiles = jnp.where(col_ok[None, :, None, :], tiles, NEG_INF)
    out = []
    for case in range(3):
        rows = []
        for qi in range(C_QROWS):
            first = (0, qi, C_QROWS)[case]
            dr0 = first - qi - case * C_QROWS + NA_ROWS - 1
            band = tiles[:, :, dr0:dr0 + NA_ROWS, :]
            rows.append(jnp.pad(band, ((0, 0), (0, 0), (first, n_krows - NA_ROWS - first), (0, 0)),
                                constant_values=NEG_INF))
        out.append(jnp.stack(rows, axis=1).reshape(n_heads, C_QTOK, n_krows * GRID_W))
    return jnp.stack(out)


def _attn_c_kernel(q_ref, k0_ref, k1_ref, k2_ref, v0_ref, v1_ref, v2_ref, bias_ref, o_ref):
    k_refs = (k0_ref, k1_ref, k2_ref)
    v_refs = (v0_ref, v1_ref, v2_ref)
    for h in range(C_HG):
        c0 = h * HEAD_DIM
        q = q_ref[0, :, c0:c0 + HEAD_DIM]
        ss = []
        for j in range(C_KBLKS):
            s = lax.dot_general(q, k_refs[j][0, :, c0:c0 + HEAD_DIM], (((1,), (1,)), ((), ())),
                                preferred_element_type=f32)
            ss.append(s * SCALE + bias_ref[0, h, :, j * C_QTOK:(j + 1) * C_QTOK])
        m = jnp.max(ss[0], axis=-1, keepdims=True)
        for j in range(1, C_KBLKS):
            m = jnp.maximum(m, jnp.max(ss[j], axis=-1, keepdims=True))
        den = None
        o = None
        for j in range(C_KBLKS):
            p = jnp.exp(ss[j] - m)
            dj = jnp.sum(p, axis=-1, keepdims=True)
            oj = jnp.dot(p.astype(bf16), v_refs[j][0, :, c0:c0 + HEAD_DIM], preferred_element_type=f32)
            den = dj if den is None else den + dj
            o = oj if o is None else o + oj
        o_ref[0, :, c0:c0 + HEAD_DIM] = (o / den).astype(o_ref.dtype)


def _attn_c(qkv, bias):
    bn, t, _ = qkv.shape
    nblk = t // C_QTOK
    hgw = C_HG * HEAD_DIM
    n_hg = C_HEADS // C_HG
    k_cb = C_OUT // hgw
    v_cb = 2 * C_OUT // hgw

    def kv_spec(cb, j):
        return pl.BlockSpec((1, C_QTOK, hgw),
                            lambda g, b, i: (b, jnp.clip(i - 1, 0, nblk - C_KBLKS) + j, cb + g))

    def case(i):
        return jnp.where(i == 0, 0, jnp.where(i == nblk - 1, 2, 1))

    out = pl.pallas_call(
        _attn_c_kernel,
        grid=(n_hg, bn, nblk),
        in_specs=[pl.BlockSpec((1, C_QTOK, hgw), lambda g, b, i: (b, i, g)),
                  kv_spec(k_cb, 0), kv_spec(k_cb, 1), kv_spec(k_cb, 2),
                  kv_spec(v_cb, 0), kv_spec(v_cb, 1), kv_spec(v_cb, 2),
                  pl.BlockSpec((1, C_HG, C_QTOK, C_KBLKS * C_QTOK), lambda g, b, i: (case(i), g, 0, 0))],
        out_specs=pl.BlockSpec((1, C_QTOK, hgw), lambda g, b, i: (b, i, g)),
        out_shape=jax.ShapeDtypeStruct((bn, t, C_OUT), bf16),
        compiler_params=_cparams(("parallel", "parallel", "arbitrary")),
        name="attn_c",
    )(qkv, qkv, qkv, qkv, qkv, qkv, qkv, bias)
    return out.reshape(bn * t, C_OUT)


def _out_proj_kernel(x_ref, a_ref, w_ref, o_ref):
    o_ref[...] = x_ref[...] + jnp.dot(a_ref[...], w_ref[...], preferred_element_type=f32)


def _out_proj(x, a, w, li, *, tm=512):
    m, d = x.shape
    k = a.shape[1]
    row = lambda i: (i, 0)
    return pl.pallas_call(
        _out_proj_kernel,
        grid=(m // tm,),
        in_specs=[pl.BlockSpec((tm, d), row),
                  pl.BlockSpec((tm, k), row),
                  _resident((None, k, d), lambda i: (li, 0, 0))],
        out_specs=pl.BlockSpec((tm, d), row),
        out_shape=jax.ShapeDtypeStruct((m, d), f32),
        compiler_params=_cparams(("parallel",)),
        name="out_c",
    )(x, a, w)


def _mixer_ab(x, bn, t, g_norm, w_in, w_out, li, sink, a_biases, b_bias, perms):
    proj = _norm_proj_ab(x, g_norm, w_in, li, perms)
    a_outs = [_attn_a_group(proj, a_biases[g], g, d, bn, t) for g, (_, d) in enumerate(DILATIONS)]
    o_b = _attn_b(proj, b_bias, sink, bn, t)
    return _out_ab(x, a_outs, o_b, w_out, li)


def _mixer_c(x, bn, t, g_norm, w_in, w_out, li, c_bias):
    qkv = _norm_matmul(x, g_norm, w_in, li).reshape(bn, t, C_IN)
    return _out_proj(x, _attn_c(qkv, c_bias), w_out, li)


def _trunk(x3, w_in_ab, w_out_ab, sink_b, w_in_c, w_out_c, a_biases, b_bias, c_biases, perms,
           norm_mix, norm_ffn, w_gate, w_up, w_down, norm_final):
    bn, t, d = x3.shape
    x = x3.reshape(bn * t, d)
    for layer in range(DEPTH):
        j = layer // 2
        if layer % 2 == 0:
            x = _mixer_ab(x, bn, t, norm_mix[layer], w_in_ab, w_out_ab, j, sink_b[j], a_biases, b_bias, perms)
        else:
            x = _mixer_c(x, bn, t, norm_mix[layer], w_in_c, w_out_c, j, c_biases[j])
        x = _ffn(x, norm_ffn[layer], w_gate, w_up, w_down, layer,
                 norm_final if layer == DEPTH - 1 else None)
    return x.reshape(bn, t, d)


def _tables(t5_table, rpb_c):
    a_biases = []
    for g, (_, d) in enumerate(DILATIONS):
        cols = t5_table[:, g * A_HEADS_PER_GROUP:(g + 1) * A_HEADS_PER_GROUP]
        a_biases.append(_a_bias(cols, d))
    b_bias = _band_bias(t5_table[:, A_HEADS:], B_SUB, B_HALF_WINDOW, 1)
    b_bias = b_bias.reshape(B_KV_HEADS, B_REP * B_SUB, B_KEYS)
    c_biases = [_na_bias(rpb_c[j]) for j in range(rpb_c.shape[0])]
    perms = jnp.asarray(np.stack([_deinterleave_perm(d) for _, d in DILATIONS[1:]]), bf16)
    return a_biases, b_bias, c_biases, perms


def kernel(x_prompt, x_sample, w_in_ab, w_out_ab, sink_b, w_in_c, w_out_c, rpb_c, t5_table, norm_mix, norm_ffn, w_gate, w_up, w_down, norm_final):
    a_biases, b_bias, c_biases, perms = _tables(t5_table, rpb_c)
    weights = (w_in_ab.astype(bf16), w_out_ab.astype(bf16), sink_b, w_in_c.astype(bf16), w_out_c.astype(bf16),
               a_biases, b_bias, c_biases, perms, norm_mix, norm_ffn,
               w_gate.astype(bf16), w_up.astype(bf16), w_down.astype(bf16), norm_final)
    return (_trunk(x_prompt, *weights), _trunk(x_sample, *weights))
```

```python
import functools

import numpy as np
import jax
import jax.numpy as jnp
from jax import lax
from jax.experimental import pallas as pl
from jax.experimental.pallas import tpu as pltpu

D_MODEL = 2048
DEPTH = 4
HEAD_DIM = 128
DILATIONS = ((128, 1), (512, 4), (2048, 16))
A_HEADS_PER_GROUP = 4
A_HEADS = A_HEADS_PER_GROUP * len(DILATIONS)
B_Q_HEADS = 8
B_KV_HEADS = 2
B_REP = B_Q_HEADS // B_KV_HEADS
B_HALF_WINDOW = 128
C_HEADS = 16
GRID_W = 64
NA_ROWS = 8
NA_COLS = 16
NUM_BUCKETS = 32
MAX_DISTANCE = 1024
RMS_EPS = 1e-6
A_GROUP_IN = 3 * A_HEADS_PER_GROUP * HEAD_DIM
A_IN = len(DILATIONS) * A_GROUP_IN
B_IN = (B_Q_HEADS + 2 * B_KV_HEADS) * HEAD_DIM
AB_IN = A_IN + B_IN
N_AB_SLABS = AB_IN // A_GROUP_IN
A_OUT = A_HEADS_PER_GROUP * HEAD_DIM
A_OUT_W = A_OUT + HEAD_DIM
B_OUT = B_Q_HEADS * HEAD_DIM
AB_OUT = A_OUT + B_OUT
C_IN = 3 * C_HEADS * HEAD_DIM
C_OUT = C_HEADS * HEAD_DIM
SCALE = HEAD_DIM ** -0.5
NEG_INF = -1e30

A_HALF = 64
A_CHUNK = 1024
A_SUB = 128
MXU_WIDTH = 256
PERM_PIECE = 256
NORM_BLOCK = 256
B_SUB = 128
B_KEYS = B_SUB + 2 * B_HALF_WINDOW
C_QROWS = 4
C_QTOK = C_QROWS * GRID_W
C_KBLKS = 3
C_HG = 4

VMEM_LIMIT = 52 * 1024 * 1024
FFN_VMEM_LIMIT = 58 * 1024 * 1024

f32 = jnp.float32
bf16 = jnp.bfloat16


def _cparams(sem, vmem_limit=VMEM_LIMIT):
    return pltpu.CompilerParams(dimension_semantics=sem, vmem_limit_bytes=vmem_limit)


def _resident(shape, index_map):
    return pl.BlockSpec(shape, index_map, pipeline_mode=pl.Buffered(1))


def _rms_rows(x_ref, g_ref, dst_ref, rows, chunk=64):
    def body(c, carry):
        r = pl.multiple_of(c * chunk, chunk)
        x = x_ref[pl.ds(r, chunk), :]
        ms = jnp.mean(x * x, axis=-1, keepdims=True)
        dst_ref[pl.ds(r, chunk), :] = ((x * lax.rsqrt(ms + RMS_EPS)) * g_ref[...]).astype(dst_ref.dtype)
        return carry
    lax.fori_loop(0, rows // chunk, body, 0, unroll=2)


def _rms_block(x_ref, g_ref, dst_ref, r0, rows, piece=64):
    for r in range(r0, r0 + rows, piece):
        x = x_ref[r:r + piece, :]
        ms = jnp.mean(x * x, axis=-1, keepdims=True)
        dst_ref[r:r + piece, :] = ((x * lax.rsqrt(ms + RMS_EPS)) * g_ref[...]).astype(dst_ref.dtype)


def _norm_matmul_kernel(x_ref, g_ref, w_ref, o_ref, h_ref, *, tm):
    j = pl.program_id(1)

    @pl.when(j == 0)
    def _():
        for r0 in range(0, tm, NORM_BLOCK):
            _rms_block(x_ref, g_ref, h_ref, r0, NORM_BLOCK)
            o_ref[r0:r0 + NORM_BLOCK, :] = jnp.dot(h_ref[r0:r0 + NORM_BLOCK, :], w_ref[...],
                                                   preferred_element_type=f32).astype(o_ref.dtype)

    @pl.when(j > 0)
    def _():
        o_ref[...] = jnp.dot(h_ref[...], w_ref[...], preferred_element_type=f32).astype(o_ref.dtype)


def _norm_matmul(x, g, w, li, *, tm=1024, tn=1536):
    m, d = x.shape
    n = w.shape[2]
    return pl.pallas_call(
        functools.partial(_norm_matmul_kernel, tm=tm),
        grid=(m // tm, n // tn),
        in_specs=[pl.BlockSpec((tm, d), lambda i, j: (i, 0)),
                  _resident((1, d), lambda i, j: (0, 0)),
                  pl.BlockSpec((None, d, tn), lambda i, j: (li, 0, j))],
        out_specs=pl.BlockSpec((tm, tn), lambda i, j: (i, j)),
        out_shape=jax.ShapeDtypeStruct((m, n), bf16),
        scratch_shapes=[pltpu.VMEM((tm, d), bf16)],
        compiler_params=_cparams(("parallel", "arbitrary")),
        name="norm_proj",
    )(x, g.reshape(1, d), w)


def _deinterleave_perm(d):
    per = PERM_PIECE // d
    p = np.zeros((PERM_PIECE, PERM_PIECE), np.float32)
    for r in range(d):
        for v in range(per):
            p[r * per + v, v * d + r] = 1.0
    return p


def _norm_proj_ab_kernel(x_ref, g_ref, w_ref, perm_ref, o_ref, h_ref, *, tm, steps_per_slab):
    j = pl.program_id(1)
    slab = j // steps_per_slab

    def project(r0, rows):
        return jnp.dot(h_ref[r0:r0 + rows, :], w_ref[...], preferred_element_type=f32).astype(bf16)

    @pl.when(j == 0)
    def _():
        for r0 in range(0, tm, NORM_BLOCK):
            _rms_block(x_ref, g_ref, h_ref, r0, NORM_BLOCK)
            o_ref[0, r0:r0 + NORM_BLOCK, :] = project(r0, NORM_BLOCK)

    @pl.when(jnp.logical_and(j > 0, jnp.logical_or(slab == 0, slab == N_AB_SLABS - 1)))
    def _():
        o_ref[0] = project(0, tm)

    for g, (_, d) in enumerate(DILATIONS):
        if d == 1:
            continue

        @pl.when(slab == g)
        def _(g=g, d=d):
            t = project(0, tm)
            run = tm // d
            per = PERM_PIECE // d
            for p in range(tm // PERM_PIECE):
                piece = jnp.dot(perm_ref[g - 1], t[p * PERM_PIECE:(p + 1) * PERM_PIECE, :],
                                preferred_element_type=f32).astype(bf16)
                for r in range(d):
                    o_ref[0, r * run + p * per:r * run + (p + 1) * per, :] = piece[r * per:(r + 1) * per, :]


def _norm_proj_ab(x, g, w, li, perms, *, tn=1536):
    m, d = x.shape
    tm = A_CHUNK
    steps_per_slab = A_GROUP_IN // tn
    return pl.pallas_call(
        functools.partial(_norm_proj_ab_kernel, tm=tm, steps_per_slab=steps_per_slab),
        grid=(m // tm, AB_IN // tn),
        in_specs=[pl.BlockSpec((tm, d), lambda i, j: (i, 0)),
                  _resident((1, d), lambda i, j: (0, 0)),
                  pl.BlockSpec((None, d, tn), lambda i, j: (li, 0, j)),
                  _resident(perms.shape, lambda i, j: (0, 0, 0))],
        out_specs=pl.BlockSpec((1, tm, tn), lambda i, j: (j // steps_per_slab, i, j % steps_per_slab)),
        out_shape=jax.ShapeDtypeStruct((N_AB_SLABS, m, A_GROUP_IN), bf16),
        scratch_shapes=[pltpu.VMEM((tm, d), bf16)],
        compiler_params=_cparams(("parallel", "arbitrary")),
        name="norm_proj_ab",
    )(x, g.reshape(1, d), w, perms)


def _ffn_kernel(x_ref, g_ref, wg_ref, wu_ref, wd_ref, *rest, tm, nf, final):
    if final:
        gf_ref, o_ref, h_ref = rest
    else:
        o_ref, h_ref = rest
    f = pl.program_id(1)

    def swiglu_rows(r0, rows):
        h = h_ref[r0:r0 + rows, :]
        gate = jnp.dot(h, wg_ref[...], preferred_element_type=f32)
        up = jnp.dot(h, wu_ref[...], preferred_element_type=f32)
        act = (gate * jax.nn.sigmoid(gate) * up).astype(bf16)
        return jnp.dot(act, wd_ref[...], preferred_element_type=f32)

    @pl.when(f == 0)
    def _():
        for r0 in range(0, tm, NORM_BLOCK):
            _rms_block(x_ref, g_ref, h_ref, r0, NORM_BLOCK)
            o_ref[r0:r0 + NORM_BLOCK, :] = x_ref[r0:r0 + NORM_BLOCK, :] + swiglu_rows(r0, NORM_BLOCK)

    @pl.when(f > 0)
    def _():
        o_ref[...] += swiglu_rows(0, tm)

    if final:
        @pl.when(f == nf - 1)
        def _():
            _rms_rows(o_ref, gf_ref, o_ref, tm)


def _ffn(x, g, wg, wu, wd, li, g_final=None, *, tm=1024, tf=512):
    m, d = x.shape
    dff = wg.shape[2]
    nf = dff // tf
    final = g_final is not None
    in_specs = [pl.BlockSpec((tm, d), lambda i, f: (i, 0)),
                _resident((1, d), lambda i, f: (0, 0)),
                pl.BlockSpec((None, d, tf), lambda i, f: (li, 0, f)),
                pl.BlockSpec((None, d, tf), lambda i, f: (li, 0, f)),
                pl.BlockSpec((None, tf, d), lambda i, f: (li, f, 0))]
    args = [x, g.reshape(1, d), wg, wu, wd]
    if final:
        in_specs.append(_resident((1, d), lambda i, f: (0, 0)))
        args.append(g_final.reshape(1, d))
    return pl.pallas_call(
        functools.partial(_ffn_kernel, tm=tm, nf=nf, final=final),
        grid=(m // tm, nf),
        in_specs=in_specs,
        out_specs=pl.BlockSpec((tm, d), lambda i, f: (i, 0)),
        out_shape=jax.ShapeDtypeStruct((m, d), f32),
        scratch_shapes=[pltpu.VMEM((tm, d), bf16)],
        compiler_params=_cparams(("parallel", "arbitrary"), FFN_VMEM_LIMIT),
        name="ffn",
    )(*args)


def _t5_bucket(rel):
    nb = NUM_BUCKETS // 2
    max_exact = nb // 2
    ret = (rel > 0).astype(np.int32) * nb
    n = np.abs(rel)
    large = max_exact + (np.log(np.maximum(n, 1) / max_exact) / np.log(MAX_DISTANCE / max_exact)
                         * (nb - max_exact)).astype(np.int32)
    large = np.minimum(large, nb - 1)
    return (ret + np.where(n < max_exact, n, large)).astype(np.int32)


def _band_bias(table_cols, n_q, half, dilation):
    n_k = n_q + 2 * half
    w = n_q + n_k
    rel = np.arange(w) - (n_q - 1) - half
    diag = table_cols[_t5_bucket(rel * dilation)].astype(f32)
    diag = jnp.where((np.abs(rel) <= half)[:, None], diag, NEG_INF).T
    flat = jnp.tile(diag, (1, n_q + 1))[:, n_q - 1:n_q - 1 + n_q * (w - 1)]
    return flat.reshape(-1, n_q, w - 1)[:, :, :n_k]


def _a_geometry(d):
    run = A_CHUNK // d
    stride = run + 2 * A_HALF
    sub = min(run, A_SUB)
    n_keys = -(-(sub + 2 * A_HALF) // MXU_WIDTH) * MXU_WIDTH
    pad = (n_keys - sub - 2 * A_HALF) // 2
    return run, stride, sub, n_keys, pad


def _a_bias(table_cols, d):
    run, stride, sub, n_keys, pad = _a_geometry(d)
    band = _band_bias(table_cols, sub, A_HALF, d)
    band = jnp.pad(band, ((0, 0), (0, 0), (pad, pad)), constant_values=NEG_INF)
    return jnp.tile(band, (1, A_SUB // sub, 1))


def _attn_a_kernel(q_ref, kp_ref, kc_ref, kn_ref, vp_ref, vc_ref, vn_ref, bias_ref, o_ref,
                   kbuf, vbuf, *, d, seq_len):
    run, stride, sub, n_keys, pad = _a_geometry(d)
    n_sb = run // sub
    c = pl.program_id(1)
    for buf, p_ref, c_ref, n_ref in ((kbuf, kp_ref, kc_ref, kn_ref), (vbuf, vp_ref, vc_ref, vn_ref)):
        if pad:
            buf[0:pad] = jnp.zeros((pad, A_OUT), bf16)
            buf[pad + d * stride:] = jnp.zeros((pad, A_OUT), bf16)
        for r in range(d):
            base = pad + r * stride
            buf[base:base + A_HALF] = p_ref[(r + 1) * run - A_HALF:(r + 1) * run]
            buf[base + A_HALF:base + A_HALF + run] = c_ref[r * run:(r + 1) * run]
            buf[base + A_HALF + run:base + stride] = n_ref[r * run:r * run + A_HALF]
    pack = A_SUB // sub
    lane = lax.broadcasted_iota(jnp.int32, (A_SUB, HEAD_DIM), 1)
    row = lax.broadcasted_iota(jnp.int32, (A_SUB, 1), 0)
    own = [row >= p * sub for p in range(pack)]

    def pick(parts):
        out = parts[0]
        for p in range(1, pack):
            out = jnp.where(own[p], parts[p], out)
        return out

    for u in range(A_CHUNK // A_SUB):
        r, sb = divmod(u * pack, n_sb)
        q0 = u * A_SUB
        k0s = [(r + p) * stride + sb * sub for p in range(pack)]
        kpos = c * run + (sb * sub - A_HALF - pad) + lax.broadcasted_iota(jnp.int32, (1, n_keys), 1)
        valid = jnp.where(kpos >= 0, kpos, seq_len) < seq_len
        lse_tile = jnp.zeros((A_SUB, HEAD_DIM), f32)
        for h in range(A_HEADS_PER_GROUP):
            c0 = h * HEAD_DIM
            q = q_ref[q0:q0 + A_SUB, c0:c0 + HEAD_DIM]
            s = pick([lax.dot_general(q, kbuf[k0:k0 + n_keys, c0:c0 + HEAD_DIM], (((1,), (1,)), ((), ())),
                                      preferred_element_type=f32) for k0 in k0s])
            s = s * SCALE + bias_ref[h]
            s = jnp.where(valid, s, NEG_INF)
            m = jnp.max(s, axis=-1, keepdims=True)
            p = jnp.exp(s - m)
            den = jnp.sum(p, axis=-1, keepdims=True)
            pb = p.astype(bf16)
            o = pick([jnp.dot(pb, vbuf[k0:k0 + n_keys, c0:c0 + HEAD_DIM], preferred_element_type=f32)
                      for k0 in k0s]) / den
            o_ref[0, q0:q0 + A_SUB, c0:c0 + HEAD_DIM] = o
            lse_tile = jnp.where(lane == h, m + jnp.log(den), lse_tile)
        o_ref[0, q0:q0 + A_SUB, A_OUT:] = lse_tile


def _attn_a_group(proj, bias, g, d, bn, t):
    run, stride, sub, n_keys, pad = _a_geometry(d)
    nc = t // A_CHUNK
    pv = proj.reshape(N_AB_SLABS, bn, t, A_GROUP_IN)

    def spec(col, which):
        def idx(b, c):
            cc = (jnp.maximum(c - 1, 0), c, jnp.minimum(c + 1, nc - 1))[which]
            return (g, b, cc, col)
        return pl.BlockSpec((None, None, A_CHUNK, A_OUT), idx)

    return pl.pallas_call(
        functools.partial(_attn_a_kernel, d=d, seq_len=t // d),
        grid=(bn, nc),
        in_specs=[spec(0, 1), spec(1, 0), spec(1, 1), spec(1, 2), spec(2, 0), spec(2, 1), spec(2, 2),
                  _resident((A_HEADS_PER_GROUP, A_SUB, n_keys), lambda b, c: (0, 0, 0))],
        out_specs=pl.BlockSpec((1, A_CHUNK, A_OUT_W), lambda b, c: (b, c, 0)),
        out_shape=jax.ShapeDtypeStruct((bn, t, A_OUT_W), f32),
        scratch_shapes=[pltpu.VMEM((d * stride + 2 * pad, A_OUT), bf16),
                        pltpu.VMEM((d * stride + 2 * pad, A_OUT), bf16)],
        compiler_params=_cparams(("parallel", "arbitrary")),
        name=f"attn_a{g}",
    )(pv, pv, pv, pv, pv, pv, pv, bias)


def _attn_b_kernel(sink_ref, cur_ref, hp_ref, hn_ref, bias_ref, o_ref, kbuf, vbuf, *, tq, seq_len):
    i = pl.program_id(1)
    kw = B_KV_HEADS * HEAD_DIM
    k_off = B_OUT
    v_off = B_OUT + kw
    hw = B_HALF_WINDOW
    kbuf[0:hw] = hp_ref[:, 0:kw]
    kbuf[hw:hw + tq] = cur_ref[:, k_off:k_off + kw]
    kbuf[hw + tq:] = hn_ref[:, 0:kw]
    vbuf[0:hw] = hp_ref[:, kw:2 * kw]
    vbuf[hw:hw + tq] = cur_ref[:, v_off:v_off + kw]
    vbuf[hw + tq:] = hn_ref[:, kw:2 * kw]
    for sb in range(tq // B_SUB):
        r0 = sb * B_SUB
        kpos = i * tq + (r0 - hw) + lax.broadcasted_iota(jnp.int32, (1, B_KEYS), 1)
        valid = jnp.where(kpos >= 0, kpos, seq_len) < seq_len
        for g in range(B_KV_HEADS):
            qs = jnp.concatenate(
                [cur_ref[r0:r0 + B_SUB, (g * B_REP + hh) * HEAD_DIM:(g * B_REP + hh + 1) * HEAD_DIM]
                 for hh in range(B_REP)], axis=0)
            k = kbuf[r0:r0 + B_KEYS, g * HEAD_DIM:(g + 1) * HEAD_DIM]
            v = vbuf[r0:r0 + B_KEYS, g * HEAD_DIM:(g + 1) * HEAD_DIM]
            s_all = lax.dot_general(qs, k, (((1,), (1,)), ((), ())), preferred_element_type=f32)
            s_all = s_all * SCALE + bias_ref[g]
            s_all = jnp.where(valid, s_all, NEG_INF)
            ps, dens = [], []
            for hh in range(B_REP):
                s = s_all[hh * B_SUB:(hh + 1) * B_SUB]
                sk = sink_ref[g * B_REP + hh]
                m = jnp.maximum(jnp.max(s, axis=-1, keepdims=True), sk)
                p = jnp.exp(s - m)
                dens.append(jnp.sum(p, axis=-1, keepdims=True) + jnp.exp(sk - m))
                ps.append(p.astype(bf16))
            o_all = jnp.dot(jnp.concatenate(ps, axis=0), v, preferred_element_type=f32)
            for hh in range(B_REP):
                c0 = (g * B_REP + hh) * HEAD_DIM
                o = o_all[hh * B_SUB:(hh + 1) * B_SUB] / dens[hh]
                o_ref[0, r0:r0 + B_SUB, c0:c0 + HEAD_DIM] = o.astype(o_ref.dtype)


def _attn_b(proj, bias, sink, bn, t):
    tq = 256
    nblk = t // tq
    hb = tq // B_HALF_WINDOW
    n_halo = t // B_HALF_WINDOW
    kvw = 2 * B_KV_HEADS * HEAD_DIM
    slab = N_AB_SLABS - 1
    pv = proj.reshape(N_AB_SLABS, bn, t, B_IN)
    out = pl.pallas_call(
        functools.partial(_attn_b_kernel, tq=tq, seq_len=t),
        grid=(bn, nblk),
        in_specs=[pl.BlockSpec(memory_space=pltpu.SMEM),
                  pl.BlockSpec((None, None, tq, B_IN), lambda b, i: (slab, b, i, 0)),
                  pl.BlockSpec((None, None, B_HALF_WINDOW, kvw),
                               lambda b, i: (slab, b, jnp.maximum(i * hb - 1, 0), B_IN // kvw - 1)),
                  pl.BlockSpec((None, None, B_HALF_WINDOW, kvw),
                               lambda b, i: (slab, b, jnp.minimum((i + 1) * hb, n_halo - 1), B_IN // kvw - 1)),
                  _resident((B_KV_HEADS, B_REP * B_SUB, B_KEYS), lambda b, i: (0, 0, 0))],
        out_specs=pl.BlockSpec((1, tq, B_OUT), lambda b, i: (b, i, 0)),
        out_shape=jax.ShapeDtypeStruct((bn, t, B_OUT), bf16),
        scratch_shapes=[pltpu.VMEM((tq + 2 * B_HALF_WINDOW, B_KV_HEADS * HEAD_DIM), bf16),
                        pltpu.VMEM((tq + 2 * B_HALF_WINDOW, B_KV_HEADS * HEAD_DIM), bf16)],
        compiler_params=_cparams(("parallel", "arbitrary")),
        name="attn_b",
    )(sink, pv, pv, pv, bias)
    return out.reshape(bn * t, B_OUT)


def _out_ab_kernel(x_ref, a0_ref, a1_ref, a2_ref, ob_ref, w_ref, o_ref, ilv_ref, mix_ref, *, tm):
    n_slabs = A_OUT_W // HEAD_DIM
    for gi, (a_ref, (_, d)) in enumerate(((a1_ref, DILATIONS[1]), (a2_ref, DILATIONS[2]))):
        per = tm // d
        for r in range(d):
            for s in range(n_slabs):
                ilv_ref[gi, s, pl.ds(r, per, stride=d), :] = a_ref[0, r, :, s * HEAD_DIM:(s + 1) * HEAD_DIM]
    for h in range(A_HEADS_PER_GROUP):
        c0 = h * HEAD_DIM
        outs = [a0_ref[:, c0:c0 + HEAD_DIM], ilv_ref[0, h], ilv_ref[1, h]]
        lses = [a0_ref[:, A_OUT + h:A_OUT + h + 1],
                ilv_ref[0, n_slabs - 1, :, h:h + 1],
                ilv_ref[1, n_slabs - 1, :, h:h + 1]]
        mx = jnp.maximum(jnp.maximum(lses[0], lses[1]), lses[2])
        es = [jnp.exp(l - mx) for l in lses]
        tot = es[0] + es[1] + es[2]
        acc = (es[0] / tot) * outs[0]
        acc += (es[1] / tot) * outs[1]
        acc += (es[2] / tot) * outs[2]
        mix_ref[:, c0:c0 + HEAD_DIM] = acc.astype(bf16)
    y = jnp.dot(ob_ref[...], w_ref[A_OUT:, :], preferred_element_type=f32)
    y += jnp.dot(mix_ref[...], w_ref[:A_OUT, :], preferred_element_type=f32)
    o_ref[...] = x_ref[...] + y


def _out_ab(x, a_outs, o_b, w, li, *, tm=512):
    m, d = x.shape
    row = lambda i: (i, 0)
    tiles = A_CHUNK // tm
    a0 = a_outs[0].reshape(m, A_OUT_W)
    dil = [dd for _, dd in DILATIONS]
    a1 = a_outs[1].reshape(m // A_CHUNK, dil[1], A_CHUNK // dil[1], A_OUT_W)
    a2 = a_outs[2].reshape(m // A_CHUNK, dil[2], A_CHUNK // dil[2], A_OUT_W)

    def deint(dd):
        return pl.BlockSpec((1, dd, tm // dd, A_OUT_W), lambda i: (i // tiles, 0, i % tiles, 0))

    return pl.pallas_call(
        functools.partial(_out_ab_kernel, tm=tm),
        grid=(m // tm,),
        in_specs=[pl.BlockSpec((tm, d), row),
                  pl.BlockSpec((tm, A_OUT_W), row),
                  deint(dil[1]),
                  deint(dil[2]),
                  pl.BlockSpec((tm, B_OUT), row),
                  _resident((None, AB_OUT, d), lambda i: (li, 0, 0))],
        out_specs=pl.BlockSpec((tm, d), row),
        out_shape=jax.ShapeDtypeStruct((m, d), f32),
        scratch_shapes=[pltpu.VMEM((2, A_OUT_W // HEAD_DIM, tm, HEAD_DIM), f32),
                        pltpu.VMEM((tm, A_OUT), bf16)],
        compiler_params=_cparams(("parallel",)),
        name="out_ab",
    )(x, a0, a1, a2, o_b, w)


def _na_bias(rpb):
    n_heads = rpb.shape[0]
    n_krows = C_KBLKS * C_QROWS
    c = np.arange(GRID_W)
    cs = np.clip(c - NA_COLS // 2, 0, GRID_W - NA_COLS)
    col_ok = (c[None, :] >= cs[:, None]) & (c[None, :] < cs[:, None] + NA_COLS)
    dc = np.clip(c[None, :] - c[:, None] + NA_COLS - 1, 0, 2 * NA_COLS - 2)
    onehot = (dc[None] == np.arange(2 * NA_COLS - 1)[:, None, None]).astype(np.float32)
    tiles = jnp.einsum('hdc,cqk->hqdk', rpb.astype(f32), onehot, precision=lax.Precision.HIGHEST)
    tiles = jnp.where(col_ok[None, :, None, :], tiles, NEG_INF)
    out = []
    for case in range(3):
        rows = []
        for qi in range(C_QROWS):
            first = (0, qi, C_QROWS)[case]
            dr0 = first - qi - case * C_QROWS + NA_ROWS - 1
            band = tiles[:, :, dr0:dr0 + NA_ROWS, :]
            rows.append(jnp.pad(band, ((0, 0), (0, 0), (first, n_krows - NA_ROWS - first), (0, 0)),
                                constant_values=NEG_INF))
        out.append(jnp.stack(rows, axis=1).reshape(n_heads, C_QTOK, n_krows * GRID_W))
    return jnp.stack(out)


def _attn_c_kernel(q_ref, k0_ref, k1_ref, k2_ref, v0_ref, v1_ref, v2_ref, bias_ref, o_ref):
    k_refs = (k0_ref, k1_ref, k2_ref)
    v_refs = (v0_ref, v1_ref, v2_ref)
    for h in range(C_HG):
        c0 = h * HEAD_DIM
        q = q_ref[0, :, c0:c0 + HEAD_DIM]
        ss = []
        for j in range(C_KBLKS):
            s = lax.dot_general(q, k_refs[j][0, :, c0:c0 + HEAD_DIM], (((1,), (1,)), ((), ())),
                                preferred_element_type=f32)
            ss.append(s * SCALE + bias_ref[0, h, :, j * C_QTOK:(j + 1) * C_QTOK])
        m = jnp.max(ss[0], axis=-1, keepdims=True)
        for j in range(1, C_KBLKS):
            m = jnp.maximum(m, jnp.max(ss[j], axis=-1, keepdims=True))
        den = None
        o = None
        for j in range(C_KBLKS):
            p = jnp.exp(ss[j] - m)
            dj = jnp.sum(p, axis=-1, keepdims=True)
            oj = jnp.dot(p.astype(bf16), v_refs[j][0, :, c0:c0 + HEAD_DIM], preferred_element_type=f32)
            den = dj if den is None else den + dj
            o = oj if o is None else o + oj
        o_ref[0, :, c0:c0 + HEAD_DIM] = (o / den).astype(o_ref.dtype)


def _attn_c(qkv, bias):
    bn, t, _ = qkv.shape
    nblk = t // C_QTOK
    hgw = C_HG * HEAD_DIM
    n_hg = C_HEADS // C_HG
    k_cb = C_OUT // hgw
    v_cb = 2 * C_OUT // hgw

    def kv_spec(cb, j):
        return pl.BlockSpec((1, C_QTOK, hgw),
                            lambda g, b, i: (b, jnp.clip(i - 1, 0, nblk - C_KBLKS) + j, cb + g))

    def case(i):
        return jnp.where(i == 0, 0, jnp.where(i == nblk - 1, 2, 1))

    out = pl.pallas_call(
        _attn_c_kernel,
        grid=(n_hg, bn, nblk),
        in_specs=[pl.BlockSpec((1, C_QTOK, hgw), lambda g, b, i: (b, i, g)),
                  kv_spec(k_cb, 0), kv_spec(k_cb, 1), kv_spec(k_cb, 2),
                  kv_spec(v_cb, 0), kv_spec(v_cb, 1), kv_spec(v_cb, 2),
                  pl.BlockSpec((1, C_HG, C_QTOK, C_KBLKS * C_QTOK), lambda g, b, i: (case(i), g, 0, 0))],
        out_specs=pl.BlockSpec((1, C_QTOK, hgw), lambda g, b, i: (b, i, g)),
        out_shape=jax.ShapeDtypeStruct((bn, t, C_OUT), bf16),
        compiler_params=_cparams(("parallel", "parallel", "arbitrary")),
        name="attn_c",
    )(qkv, qkv, qkv, qkv, qkv, qkv, qkv, bias)
    return out.reshape(bn * t, C_OUT)


def _out_proj_kernel(x_ref, a_ref, w_ref, o_ref):
    o_ref[...] = x_ref[...] + jnp.dot(a_ref[...], w_ref[...], preferred_element_type=f32)


def _out_proj(x, a, w, li, *, tm=512):
    m, d = x.shape
    k = a.shape[1]
    row = lambda i: (i, 0)
    return pl.pallas_call(
        _out_proj_kernel,
        grid=(m // tm,),
        in_specs=[pl.BlockSpec((tm, d), row),
                  pl.BlockSpec((tm, k), row),
                  _resident((None, k, d), lambda i: (li, 0, 0))],
        out_specs=pl.BlockSpec((tm, d), row),
        out_shape=jax.ShapeDtypeStruct((m, d), f32),
        compiler_params=_cparams(("parallel",)),
        name="out_c",
    )(x, a, w)


def _mixer_ab(x, bn, t, g_norm, w_in, w_out, li, sink, a_biases, b_bias, perms):
    proj = _norm_proj_ab(x, g_norm, w_in, li, perms)
    a_outs = [_attn_a_group(proj, a_biases[g], g, d, bn, t) for g, (_, d) in enumerate(DILATIONS)]
    o_b = _attn_b(proj, b_bias, sink, bn, t)
    return _out_ab(x, a_outs, o_b, w_out, li)


def _mixer_c(x, bn, t, g_norm, w_in, w_out, li, c_bias):
    qkv = _norm_matmul(x, g_norm, w_in, li).reshape(bn, t, C_IN)
    return _out_proj(x, _attn_c(qkv, c_bias), w_out, li)


def _trunk(x3, w_in_ab, w_out_ab, sink_b, w_in_c, w_out_c, a_biases, b_bias, c_biases, perms,
           norm_mix, norm_ffn, w_gate, w_up, w_down, norm_final):
    bn, t, d = x3.shape
    x = x3.reshape(bn * t, d)
    for layer in range(DEPTH):
        j = layer // 2
        if layer % 2 == 0:
            x = _mixer_ab(x, bn, t, norm_mix[layer], w_in_ab, w_out_ab, j, sink_b[j], a_biases, b_bias, perms)
        else:
            x = _mixer_c(x, bn, t, norm_mix[layer], w_in_c, w_out_c, j, c_biases[j])
        x = _ffn(x, norm_ffn[layer], w_gate, w_up, w_down, layer,
                 norm_final if layer == DEPTH - 1 else None)
    return x.reshape(bn, t, d)


def _tables(t5_table, rpb_c):
    a_biases = []
    for g, (_, d) in enumerate(DILATIONS):
        cols = t5_table[:, g * A_HEADS_PER_GROUP:(g + 1) * A_HEADS_PER_GROUP]
        a_biases.append(_a_bias(cols, d))
    b_bias = _band_bias(t5_table[:, A_HEADS:], B_SUB, B_HALF_WINDOW, 1)
    b_bias = b_bias.reshape(B_KV_HEADS, B_REP * B_SUB, B_KEYS)
    c_biases = [_na_bias(rpb_c[j]) for j in range(rpb_c.shape[0])]
    perms = jnp.asarray(np.stack([_deinterleave_perm(d) for _, d in DILATIONS[1:]]), bf16)
    return a_biases, b_bias, c_biases, perms


def kernel(x_prompt, x_sample, w_in_ab, w_out_ab, sink_b, w_in_c, w_out_c, rpb_c, t5_table, norm_mix, norm_ffn, w_gate, w_up, w_down, norm_final):
    a_biases, b_bias, c_biases, perms = _tables(t5_table, rpb_c)
    weights = (w_in_ab.astype(bf16), w_out_ab.astype(bf16), sink_b, w_in_c.astype(bf16), w_out_c.astype(bf16),
               a_biases, b_bias, c_biases, perms, norm_mix, norm_ffn,
               w_gate.astype(bf16), w_up.astype(bf16), w_down.astype(bf16), norm_final)
    return (_trunk(x_prompt, *weights), _trunk(x_sample, *weights))
```

```python
import functools

import numpy as np
import jax
import jax.numpy as jnp
from jax import lax
from jax.experimental import pallas as pl
from jax.experimental.pallas import tpu as pltpu

D_MODEL = 2048
DEPTH = 4
HEAD_DIM = 128
DILATIONS = ((128, 1), (512, 4), (2048, 16))
A_HEADS_PER_GROUP = 4
A_HEADS = A_HEADS_PER_GROUP * len(DILATIONS)
B_Q_HEADS = 8
B_KV_HEADS = 2
B_REP = B_Q_HEADS // B_KV_HEADS
B_HALF_WINDOW = 128
C_HEADS = 16
GRID_W = 64
NA_ROWS = 8
NA_COLS = 16
NUM_BUCKETS = 32
MAX_DISTANCE = 1024
RMS_EPS = 1e-6
A_GROUP_IN = 3 * A_HEADS_PER_GROUP * HEAD_DIM
A_IN = len(DILATIONS) * A_GROUP_IN
B_IN = (B_Q_HEADS + 2 * B_KV_HEADS) * HEAD_DIM
AB_IN = A_IN + B_IN
N_AB_SLABS = AB_IN // A_GROUP_IN
A_OUT = A_HEADS_PER_GROUP * HEAD_DIM
A_OUT_W = A_OUT + HEAD_DIM
B_OUT = B_Q_HEADS * HEAD_DIM
AB_OUT = A_OUT + B_OUT
C_IN = 3 * C_HEADS * HEAD_DIM
C_OUT = C_HEADS * HEAD_DIM
SCALE = HEAD_DIM ** -0.5
NEG_INF = -1e30

A_HALF = 64
A_CHUNK = 1024
A_SUB = 128
PERM_PIECE = 256
NORM_BLOCK = 256
B_SUB = 128
B_KEYS = B_SUB + 2 * B_HALF_WINDOW
C_QROWS = 4
C_QTOK = C_QROWS * GRID_W
C_KBLKS = 3
C_HG = 4

VMEM_LIMIT = 52 * 1024 * 1024
FFN_VMEM_LIMIT = 58 * 1024 * 1024

f32 = jnp.float32
bf16 = jnp.bfloat16


def _cparams(sem, vmem_limit=VMEM_LIMIT):
    return pltpu.CompilerParams(dimension_semantics=sem, vmem_limit_bytes=vmem_limit)


def _resident(shape, index_map):
    return pl.BlockSpec(shape, index_map, pipeline_mode=pl.Buffered(1))


def _rms_rows(x_ref, g_ref, dst_ref, rows, chunk=64):
    def body(c, carry):
        r = pl.multiple_of(c * chunk, chunk)
        x = x_ref[pl.ds(r, chunk), :]
        ms = jnp.mean(x * x, axis=-1, keepdims=True)
        dst_ref[pl.ds(r, chunk), :] = ((x * lax.rsqrt(ms + RMS_EPS)) * g_ref[...]).astype(dst_ref.dtype)
        return carry
    lax.fori_loop(0, rows // chunk, body, 0, unroll=2)


def _rms_block(x_ref, g_ref, dst_ref, r0, rows, piece=64):
    for r in range(r0, r0 + rows, piece):
        x = x_ref[r:r + piece, :]
        ms = jnp.mean(x * x, axis=-1, keepdims=True)
        dst_ref[r:r + piece, :] = ((x * lax.rsqrt(ms + RMS_EPS)) * g_ref[...]).astype(dst_ref.dtype)


def _norm_matmul_kernel(x_ref, g_ref, w_ref, o_ref, h_ref, *, tm):
    j = pl.program_id(1)

    @pl.when(j == 0)
    def _():
        for r0 in range(0, tm, NORM_BLOCK):
            _rms_block(x_ref, g_ref, h_ref, r0, NORM_BLOCK)
            o_ref[r0:r0 + NORM_BLOCK, :] = jnp.dot(h_ref[r0:r0 + NORM_BLOCK, :], w_ref[...],
                                                   preferred_element_type=f32).astype(o_ref.dtype)

    @pl.when(j > 0)
    def _():
        o_ref[...] = jnp.dot(h_ref[...], w_ref[...], preferred_element_type=f32).astype(o_ref.dtype)


def _norm_matmul(x, g, w, li, *, tm=1024, tn=1536):
    m, d = x.shape
    n = w.shape[2]
    return pl.pallas_call(
        functools.partial(_norm_matmul_kernel, tm=tm),
        grid=(m // tm, n // tn),
        in_specs=[pl.BlockSpec((tm, d), lambda i, j: (i, 0)),
                  _resident((1, d), lambda i, j: (0, 0)),
                  pl.BlockSpec((None, d, tn), lambda i, j: (li, 0, j))],
        out_specs=pl.BlockSpec((tm, tn), lambda i, j: (i, j)),
        out_shape=jax.ShapeDtypeStruct((m, n), bf16),
        scratch_shapes=[pltpu.VMEM((tm, d), bf16)],
        compiler_params=_cparams(("parallel", "arbitrary")),
        name="norm_proj",
    )(x, g.reshape(1, d), w)


def _deinterleave_perm(d):
    per = PERM_PIECE // d
    p = np.zeros((PERM_PIECE, PERM_PIECE), np.float32)
    for r in range(d):
        for v in range(per):
            p[r * per + v, v * d + r] = 1.0
    return p


def _norm_proj_ab_kernel(x_ref, g_ref, w_ref, perm_ref, o_ref, h_ref, *, tm, steps_per_slab):
    j = pl.program_id(1)
    slab = j // steps_per_slab

    def project(r0, rows):
        return jnp.dot(h_ref[r0:r0 + rows, :], w_ref[...], preferred_element_type=f32).astype(bf16)

    @pl.when(j == 0)
    def _():
        for r0 in range(0, tm, NORM_BLOCK):
            _rms_block(x_ref, g_ref, h_ref, r0, NORM_BLOCK)
            o_ref[0, r0:r0 + NORM_BLOCK, :] = project(r0, NORM_BLOCK)

    @pl.when(jnp.logical_and(j > 0, jnp.logical_or(slab == 0, slab == N_AB_SLABS - 1)))
    def _():
        o_ref[0] = project(0, tm)

    for g, (_, d) in enumerate(DILATIONS):
        if d == 1:
            continue

        @pl.when(slab == g)
        def _(g=g, d=d):
            t = project(0, tm)
            run = tm // d
            per = PERM_PIECE // d
            for p in range(tm // PERM_PIECE):
                piece = jnp.dot(perm_ref[g - 1], t[p * PERM_PIECE:(p + 1) * PERM_PIECE, :],
                                preferred_element_type=f32).astype(bf16)
                for r in range(d):
                    o_ref[0, r * run + p * per:r * run + (p + 1) * per, :] = piece[r * per:(r + 1) * per, :]


def _norm_proj_ab(x, g, w, li, perms, *, tn=1536):
    m, d = x.shape
    tm = A_CHUNK
    steps_per_slab = A_GROUP_IN // tn
    return pl.pallas_call(
        functools.partial(_norm_proj_ab_kernel, tm=tm, steps_per_slab=steps_per_slab),
        grid=(m // tm, AB_IN // tn),
        in_specs=[pl.BlockSpec((tm, d), lambda i, j: (i, 0)),
                  _resident((1, d), lambda i, j: (0, 0)),
                  pl.BlockSpec((None, d, tn), lambda i, j: (li, 0, j)),
                  _resident(perms.shape, lambda i, j: (0, 0, 0))],
        out_specs=pl.BlockSpec((1, tm, tn), lambda i, j: (j // steps_per_slab, i, j % steps_per_slab)),
        out_shape=jax.ShapeDtypeStruct((N_AB_SLABS, m, A_GROUP_IN), bf16),
        scratch_shapes=[pltpu.VMEM((tm, d), bf16)],
        compiler_params=_cparams(("parallel", "arbitrary")),
        name="norm_proj_ab",
    )(x, g.reshape(1, d), w, perms)


def _ffn_kernel(x_ref, g_ref, wg_ref, wu_ref, wd_ref, *rest, tm, nf, final):
    if final:
        gf_ref, o_ref, h_ref = rest
    else:
        o_ref, h_ref = rest
    f = pl.program_id(1)

    def swiglu_rows(r0, rows):
        h = h_ref[r0:r0 + rows, :]
        gate = jnp.dot(h, wg_ref[...], preferred_element_type=f32)
        up = jnp.dot(h, wu_ref[...], preferred_element_type=f32)
        act = (gate * jax.nn.sigmoid(gate) * up).astype(bf16)
        return jnp.dot(act, wd_ref[...], preferred_element_type=f32)

    @pl.when(f == 0)
    def _():
        for r0 in range(0, tm, NORM_BLOCK):
            _rms_block(x_ref, g_ref, h_ref, r0, NORM_BLOCK)
            o_ref[r0:r0 + NORM_BLOCK, :] = x_ref[r0:r0 + NORM_BLOCK, :] + swiglu_rows(r0, NORM_BLOCK)

    @pl.when(f > 0)
    def _():
        o_ref[...] += swiglu_rows(0, tm)

    if final:
        @pl.when(f == nf - 1)
        def _():
            _rms_rows(o_ref, gf_ref, o_ref, tm)


def _ffn(x, g, wg, wu, wd, li, g_final=None, *, tm=1024, tf=512):
    m, d = x.shape
    dff = wg.shape[2]
    nf = dff // tf
    final = g_final is not None
    in_specs = [pl.BlockSpec((tm, d), lambda i, f: (i, 0)),
                _resident((1, d), lambda i, f: (0, 0)),
                pl.BlockSpec((None, d, tf), lambda i, f: (li, 0, f)),
                pl.BlockSpec((None, d, tf), lambda i, f: (li, 0, f)),
                pl.BlockSpec((None, tf, d), lambda i, f: (li, f, 0))]
    args = [x, g.reshape(1, d), wg, wu, wd]
    if final:
        in_specs.append(_resident((1, d), lambda i, f: (0, 0)))
        args.append(g_final.reshape(1, d))
    return pl.pallas_call(
        functools.partial(_ffn_kernel, tm=tm, nf=nf, final=final),
        grid=(m // tm, nf),
        in_specs=in_specs,
        out_specs=pl.BlockSpec((tm, d), lambda i, f: (i, 0)),
        out_shape=jax.ShapeDtypeStruct((m, d), f32),
        scratch_shapes=[pltpu.VMEM((tm, d), bf16)],
        compiler_params=_cparams(("parallel", "arbitrary"), FFN_VMEM_LIMIT),
        name="ffn",
    )(*args)


def _t5_bucket(rel):
    nb = NUM_BUCKETS // 2
    max_exact = nb // 2
    ret = (rel > 0).astype(np.int32) * nb
    n = np.abs(rel)
    large = max_exact + (np.log(np.maximum(n, 1) / max_exact) / np.log(MAX_DISTANCE / max_exact)
                         * (nb - max_exact)).astype(np.int32)
    large = np.minimum(large, nb - 1)
    return (ret + np.where(n < max_exact, n, large)).astype(np.int32)


def _band_bias(table_cols, n_q, half, dilation):
    n_k = n_q + 2 * half
    w = n_q + n_k
    rel = np.arange(w) - (n_q - 1) - half
    diag = table_cols[_t5_bucket(rel * dilation)].astype(f32)
    diag = jnp.where((np.abs(rel) <= half)[:, None], diag, NEG_INF).T
    flat = jnp.tile(diag, (1, n_q + 1))[:, n_q - 1:n_q - 1 + n_q * (w - 1)]
    return flat.reshape(-1, n_q, w - 1)[:, :, :n_k]


def _a_geometry(d):
    run = A_CHUNK // d
    n_span = max(1, A_SUB // run)
    seg = n_span * run
    stride = seg + 2 * A_HALF
    return run, n_span, seg, stride


def _attn_a_kernel(q_ref, kp_ref, kc_ref, kn_ref, vp_ref, vc_ref, vn_ref, bias_ref, o_ref,
                   kbuf, vbuf, *, d, seq_len):
    run, n_span, seg, stride = _a_geometry(d)
    piece = min(run, A_SUB)
    n_keys = A_SUB + 2 * A_HALF
    step = pl.program_id(1)
    for buf, p_ref, c_ref, n_ref in ((kbuf, kp_ref, kc_ref, kn_ref), (vbuf, vp_ref, vc_ref, vn_ref)):
        for r in range(d):
            base = r * stride
            buf[base:base + A_HALF] = p_ref[(r + 1) * run - A_HALF:(r + 1) * run]
            for sp in range(n_span):
                lo = base + A_HALF + sp * run
                buf[lo:lo + run] = c_ref[sp * A_CHUNK + r * run:sp * A_CHUNK + (r + 1) * run]
            buf[base + A_HALF + seg:base + stride] = n_ref[r * run:r * run + A_HALF]
    lane = lax.broadcasted_iota(jnp.int32, (A_SUB, HEAD_DIM), 1)

    for r in range(d):
        for sb in range(seg // A_SUB):
            pos0 = sb * A_SUB
            rows = [(p // run) * A_CHUNK + r * run + p % run for p in range(pos0, pos0 + A_SUB, piece)]
            k0 = r * stride + pos0
            kpos = step * seg + (pos0 - A_HALF) + lax.broadcasted_iota(jnp.int32, (1, n_keys), 1)
            valid = jnp.where(kpos >= 0, kpos, seq_len) < seq_len
            lse_tile = jnp.zeros((A_SUB, HEAD_DIM), f32)
            for h in range(A_HEADS_PER_GROUP):
                c0 = h * HEAD_DIM
                q = jnp.concatenate([q_ref[q0:q0 + piece, c0:c0 + HEAD_DIM] for q0 in rows], axis=0)
                k = kbuf[k0:k0 + n_keys, c0:c0 + HEAD_DIM]
                v = vbuf[k0:k0 + n_keys, c0:c0 + HEAD_DIM]
                s = lax.dot_general(q, k, (((1,), (1,)), ((), ())), preferred_element_type=f32)
                s = s * SCALE + bias_ref[h]
                s = jnp.where(valid, s, NEG_INF)
                m = jnp.max(s, axis=-1, keepdims=True)
                p = jnp.exp(s - m)
                den = jnp.sum(p, axis=-1, keepdims=True)
                o = jnp.dot(p.astype(bf16), v, preferred_element_type=f32) / den
                for i, q0 in enumerate(rows):
                    o_ref[0, q0:q0 + piece, c0:c0 + HEAD_DIM] = o[i * piece:(i + 1) * piece]
                lse_tile = jnp.where(lane == h, m + jnp.log(den), lse_tile)
            for i, q0 in enumerate(rows):
                o_ref[0, q0:q0 + piece, A_OUT:] = lse_tile[i * piece:(i + 1) * piece]


def _attn_a_group(proj, bias, g, d, bn, t):
    run, n_span, seg, stride = _a_geometry(d)
    nc = t // A_CHUNK
    pv = proj.reshape(N_AB_SLABS, bn, t, A_GROUP_IN)

    def cur(col):
        return pl.BlockSpec((None, None, n_span * A_CHUNK, A_OUT), lambda b, c: (g, b, c, col))

    def prev(col):
        return pl.BlockSpec((None, None, A_CHUNK, A_OUT), lambda b, c: (g, b, jnp.maximum(c * n_span - 1, 0), col))

    def nxt(col):
        return pl.BlockSpec((None, None, A_CHUNK, A_OUT),
                            lambda b, c: (g, b, jnp.minimum((c + 1) * n_span, nc - 1), col))

    return pl.pallas_call(
        functools.partial(_attn_a_kernel, d=d, seq_len=t // d),
        grid=(bn, nc // n_span),
        in_specs=[cur(0), prev(1), cur(1), nxt(1), prev(2), cur(2), nxt(2),
                  _resident((A_HEADS_PER_GROUP, A_SUB, A_SUB + 2 * A_HALF), lambda b, c: (0, 0, 0))],
        out_specs=pl.BlockSpec((1, n_span * A_CHUNK, A_OUT_W), lambda b, c: (b, c, 0)),
        out_shape=jax.ShapeDtypeStruct((bn, t, A_OUT_W), f32),
        scratch_shapes=[pltpu.VMEM((d * stride, A_OUT), bf16),
                        pltpu.VMEM((d * stride, A_OUT), bf16)],
        compiler_params=_cparams(("parallel", "arbitrary")),
        name=f"attn_a{g}",
    )(pv, pv, pv, pv, pv, pv, pv, bias)


def _attn_b_kernel(sink_ref, cur_ref, hp_ref, hn_ref, bias_ref, o_ref, kbuf, vbuf, *, tq, seq_len):
    i = pl.program_id(1)
    kw = B_KV_HEADS * HEAD_DIM
    k_off = B_OUT
    v_off = B_OUT + kw
    hw = B_HALF_WINDOW
    kbuf[0:hw] = hp_ref[:, 0:kw]
    kbuf[hw:hw + tq] = cur_ref[:, k_off:k_off + kw]
    kbuf[hw + tq:] = hn_ref[:, 0:kw]
    vbuf[0:hw] = hp_ref[:, kw:2 * kw]
    vbuf[hw:hw + tq] = cur_ref[:, v_off:v_off + kw]
    vbuf[hw + tq:] = hn_ref[:, kw:2 * kw]
    for sb in range(tq // B_SUB):
        r0 = sb * B_SUB
        kpos = i * tq + (r0 - hw) + lax.broadcasted_iota(jnp.int32, (1, B_KEYS), 1)
        valid = jnp.where(kpos >= 0, kpos, seq_len) < seq_len
        for g in range(B_KV_HEADS):
            qs = jnp.concatenate(
                [cur_ref[r0:r0 + B_SUB, (g * B_REP + hh) * HEAD_DIM:(g * B_REP + hh + 1) * HEAD_DIM]
                 for hh in range(B_REP)], axis=0)
            k = kbuf[r0:r0 + B_KEYS, g * HEAD_DIM:(g + 1) * HEAD_DIM]
            v = vbuf[r0:r0 + B_KEYS, g * HEAD_DIM:(g + 1) * HEAD_DIM]
            s_all = lax.dot_general(qs, k, (((1,), (1,)), ((), ())), preferred_element_type=f32)
            s_all = s_all * SCALE + bias_ref[g]
            s_all = jnp.where(valid, s_all, NEG_INF)
            ps, dens = [], []
            for hh in range(B_REP):
                s = s_all[hh * B_SUB:(hh + 1) * B_SUB]
                sk = sink_ref[g * B_REP + hh]
                m = jnp.maximum(jnp.max(s, axis=-1, keepdims=True), sk)
                p = jnp.exp(s - m)
                dens.append(jnp.sum(p, axis=-1, keepdims=True) + jnp.exp(sk - m))
                ps.append(p.astype(bf16))
            o_all = jnp.dot(jnp.concatenate(ps, axis=0), v, preferred_element_type=f32)
            for hh in range(B_REP):
                c0 = (g * B_REP + hh) * HEAD_DIM
                o = o_all[hh * B_SUB:(hh + 1) * B_SUB] / dens[hh]
                o_ref[0, r0:r0 + B_SUB, c0:c0 + HEAD_DIM] = o.astype(o_ref.dtype)


def _attn_b(proj, bias, sink, bn, t):
    tq = 256
    nblk = t // tq
    hb = tq // B_HALF_WINDOW
    n_halo = t // B_HALF_WINDOW
    kvw = 2 * B_KV_HEADS * HEAD_DIM
    slab = N_AB_SLABS - 1
    pv = proj.reshape(N_AB_SLABS, bn, t, B_IN)
    out = pl.pallas_call(
        functools.partial(_attn_b_kernel, tq=tq, seq_len=t),
        grid=(bn, nblk),
        in_specs=[pl.BlockSpec(memory_space=pltpu.SMEM),
                  pl.BlockSpec((None, None, tq, B_IN), lambda b, i: (slab, b, i, 0)),
                  pl.BlockSpec((None, None, B_HALF_WINDOW, kvw),
                               lambda b, i: (slab, b, jnp.maximum(i * hb - 1, 0), B_IN // kvw - 1)),
                  pl.BlockSpec((None, None, B_HALF_WINDOW, kvw),
                               lambda b, i: (slab, b, jnp.minimum((i + 1) * hb, n_halo - 1), B_IN // kvw - 1)),
                  _resident((B_KV_HEADS, B_REP * B_SUB, B_KEYS), lambda b, i: (0, 0, 0))],
        out_specs=pl.BlockSpec((1, tq, B_OUT), lambda b, i: (b, i, 0)),
        out_shape=jax.ShapeDtypeStruct((bn, t, B_OUT), bf16),
        scratch_shapes=[pltpu.VMEM((tq + 2 * B_HALF_WINDOW, B_KV_HEADS * HEAD_DIM), bf16),
                        pltpu.VMEM((tq + 2 * B_HALF_WINDOW, B_KV_HEADS * HEAD_DIM), bf16)],
        compiler_params=_cparams(("parallel", "arbitrary")),
        name="attn_b",
    )(sink, pv, pv, pv, bias)
    return out.reshape(bn * t, B_OUT)


def _out_ab_kernel(x_ref, a0_ref, a1_ref, a2_ref, ob_ref, w_ref, o_ref, ilv_ref, mix_ref, *, tm):
    n_slabs = A_OUT_W // HEAD_DIM
    for gi, (a_ref, (_, d)) in enumerate(((a1_ref, DILATIONS[1]), (a2_ref, DILATIONS[2]))):
        per = tm // d
        for r in range(d):
            for s in range(n_slabs):
                ilv_ref[gi, s, pl.ds(r, per, stride=d), :] = a_ref[0, r, :, s * HEAD_DIM:(s + 1) * HEAD_DIM]
    for h in range(A_HEADS_PER_GROUP):
        c0 = h * HEAD_DIM
        outs = [a0_ref[:, c0:c0 + HEAD_DIM], ilv_ref[0, h], ilv_ref[1, h]]
        lses = [a0_ref[:, A_OUT + h:A_OUT + h + 1],
                ilv_ref[0, n_slabs - 1, :, h:h + 1],
                ilv_ref[1, n_slabs - 1, :, h:h + 1]]
        mx = jnp.maximum(jnp.maximum(lses[0], lses[1]), lses[2])
        es = [jnp.exp(l - mx) for l in lses]
        tot = es[0] + es[1] + es[2]
        acc = (es[0] / tot) * outs[0]
        acc += (es[1] / tot) * outs[1]
        acc += (es[2] / tot) * outs[2]
        mix_ref[:, c0:c0 + HEAD_DIM] = acc.astype(bf16)
    y = jnp.dot(ob_ref[...], w_ref[A_OUT:, :], preferred_element_type=f32)
    y += jnp.dot(mix_ref[...], w_ref[:A_OUT, :], preferred_element_type=f32)
    o_ref[...] = x_ref[...] + y


def _out_ab(x, a_outs, o_b, w, li, *, tm=512):
    m, d = x.shape
    row = lambda i: (i, 0)
    tiles = A_CHUNK // tm
    a0 = a_outs[0].reshape(m, A_OUT_W)
    dil = [dd for _, dd in DILATIONS]
    a1 = a_outs[1].reshape(m // A_CHUNK, dil[1], A_CHUNK // dil[1], A_OUT_W)
    a2 = a_outs[2].reshape(m // A_CHUNK, dil[2], A_CHUNK // dil[2], A_OUT_W)

    def deint(dd):
        return pl.BlockSpec((1, dd, tm // dd, A_OUT_W), lambda i: (i // tiles, 0, i % tiles, 0))

    return pl.pallas_call(
        functools.partial(_out_ab_kernel, tm=tm),
        grid=(m // tm,),
        in_specs=[pl.BlockSpec((tm, d), row),
                  pl.BlockSpec((tm, A_OUT_W), row),
                  deint(dil[1]),
                  deint(dil[2]),
                  pl.BlockSpec((tm, B_OUT), row),
                  _resident((None, AB_OUT, d), lambda i: (li, 0, 0))],
        out_specs=pl.BlockSpec((tm, d), row),
        out_shape=jax.ShapeDtypeStruct((m, d), f32),
        scratch_shapes=[pltpu.VMEM((2, A_OUT_W // HEAD_DIM, tm, HEAD_DIM), f32),
                        pltpu.VMEM((tm, A_OUT), bf16)],
        compiler_params=_cparams(("parallel",)),
        name="out_ab",
    )(x, a0, a1, a2, o_b, w)


def _na_bias(rpb):
    n_heads = rpb.shape[0]
    n_krows = C_KBLKS * C_QROWS
    c = np.arange(GRID_W)
    cs = np.clip(c - NA_COLS // 2, 0, GRID_W - NA_COLS)
    col_ok = (c[None, :] >= cs[:, None]) & (c[None, :] < cs[:, None] + NA_COLS)
    dc = np.clip(c[None, :] - c[:, None] + NA_COLS - 1, 0, 2 * NA_COLS - 2)
    onehot = (dc[None] == np.arange(2 * NA_COLS - 1)[:, None, None]).astype(np.float32)
    tiles = jnp.einsum('hdc,cqk->hqdk', rpb.astype(f32), onehot, precision=lax.Precision.HIGHEST)
    tiles = jnp.where(col_ok[None, :, None, :], tiles, NEG_INF)
    out = []
    for case in range(3):
        rows = []
        for qi in range(C_QROWS):
            first = (0, qi, C_QROWS)[case]
            dr0 = first - qi - case * C_QROWS + NA_ROWS - 1
            band = tiles[:, :, dr0:dr0 + NA_ROWS, :]
            rows.append(jnp.pad(band, ((0, 0), (0, 0), (first, n_krows - NA_ROWS - first), (0, 0)),
                                constant_values=NEG_INF))
        out.append(jnp.stack(rows, axis=1).reshape(n_heads, C_QTOK, n_krows * GRID_W))
    return jnp.stack(out)


def _attn_c_kernel(q_ref, k0_ref, k1_ref, k2_ref, v0_ref, v1_ref, v2_ref, bias_ref, o_ref):
    k_refs = (k0_ref, k1_ref, k2_ref)
    v_refs = (v0_ref, v1_ref, v2_ref)
    for h in range(C_HG):
        c0 = h * HEAD_DIM
        q = q_ref[0, :, c0:c0 + HEAD_DIM]
        ss = []
        for j in range(C_KBLKS):
            s = lax.dot_general(q, k_refs[j][0, :, c0:c0 + HEAD_DIM], (((1,), (1,)), ((), ())),
                                preferred_element_type=f32)
            ss.append(s * SCALE + bias_ref[0, h, :, j * C_QTOK:(j + 1) * C_QTOK])
        m = jnp.max(ss[0], axis=-1, keepdims=True)
        for j in range(1, C_KBLKS):
            m = jnp.maximum(m, jnp.max(ss[j], axis=-1, keepdims=True))
        den = None
        o = None
        for j in range(C_KBLKS):
            p = jnp.exp(ss[j] - m)
            dj = jnp.sum(p, axis=-1, keepdims=True)
            oj = jnp.dot(p.astype(bf16), v_refs[j][0, :, c0:c0 + HEAD_DIM], preferred_element_type=f32)
            den = dj if den is None else den + dj
            o = oj if o is None else o + oj
        o_ref[0, :, c0:c0 + HEAD_DIM] = (o / den).astype(o_ref.dtype)


def _attn_c(qkv, bias):
    bn, t, _ = qkv.shape
    nblk = t // C_QTOK
    hgw = C_HG * HEAD_DIM
    n_hg = C_HEADS // C_HG
    k_cb = C_OUT // hgw
    v_cb = 2 * C_OUT // hgw

    def kv_spec(cb, j):
        return pl.BlockSpec((1, C_QTOK, hgw),
                            lambda g, b, i: (b, jnp.clip(i - 1, 0, nblk - C_KBLKS) + j, cb + g))

    def case(i):
        return jnp.where(i == 0, 0, jnp.where(i == nblk - 1, 2, 1))

    out = pl.pallas_call(
        _attn_c_kernel,
        grid=(n_hg, bn, nblk),
        in_specs=[pl.BlockSpec((1, C_QTOK, hgw), lambda g, b, i: (b, i, g)),
                  kv_spec(k_cb, 0), kv_spec(k_cb, 1), kv_spec(k_cb, 2),
                  kv_spec(v_cb, 0), kv_spec(v_cb, 1), kv_spec(v_cb, 2),
                  pl.BlockSpec((1, C_HG, C_QTOK, C_KBLKS * C_QTOK), lambda g, b, i: (case(i), g, 0, 0))],
        out_specs=pl.BlockSpec((1, C_QTOK, hgw), lambda g, b, i: (b, i, g)),
        out_shape=jax.ShapeDtypeStruct((bn, t, C_OUT), bf16),
        compiler_params=_cparams(("parallel", "parallel", "arbitrary")),
        name="attn_c",
    )(qkv, qkv, qkv, qkv, qkv, qkv, qkv, bias)
    return out.reshape(bn * t, C_OUT)


def _out_proj_kernel(x_ref, a_ref, w_ref, o_ref):
    o_ref[...] = x_ref[...] + jnp.dot(a_ref[...], w_ref[...], preferred_element_type=f32)


def _out_proj(x, a, w, li, *, tm=512):
    m, d = x.shape
    k = a.shape[1]
    row = lambda i: (i, 0)
    return pl.pallas_call(
        _out_proj_kernel,
        grid=(m // tm,),
        in_specs=[pl.BlockSpec((tm, d), row),
                  pl.BlockSpec((tm, k), row),
                  _resident((None, k, d), lambda i: (li, 0, 0))],
        out_specs=pl.BlockSpec((tm, d), row),
        out_shape=jax.ShapeDtypeStruct((m, d), f32),
        compiler_params=_cparams(("parallel",)),
        name="out_c",
    )(x, a, w)


def _mixer_ab(x, bn, t, g_norm, w_in, w_out, li, sink, a_biases, b_bias, perms):
    proj = _norm_proj_ab(x, g_norm, w_in, li, perms)
    a_outs = [_attn_a_group(proj, a_biases[g], g, d, bn, t) for g, (_, d) in enumerate(DILATIONS)]
    o_b = _attn_b(proj, b_bias, sink, bn, t)
    return _out_ab(x, a_outs, o_b, w_out, li)


def _mixer_c(x, bn, t, g_norm, w_in, w_out, li, c_bias):
    qkv = _norm_matmul(x, g_norm, w_in, li).reshape(bn, t, C_IN)
    return _out_proj(x, _attn_c(qkv, c_bias), w_out, li)


def _trunk(x3, w_in_ab, w_out_ab, sink_b, w_in_c, w_out_c, a_biases, b_bias, c_biases, perms,
           norm_mix, norm_ffn, w_gate, w_up, w_down, norm_final):
    bn, t, d = x3.shape
    x = x3.reshape(bn * t, d)
    for layer in range(DEPTH):
        j = layer // 2
        if layer % 2 == 0:
            x = _mixer_ab(x, bn, t, norm_mix[layer], w_in_ab, w_out_ab, j, sink_b[j], a_biases, b_bias, perms)
        else:
            x = _mixer_c(x, bn, t, norm_mix[layer], w_in_c, w_out_c, j, c_biases[j])
        x = _ffn(x, norm_ffn[layer], w_gate, w_up, w_down, layer,
                 norm_final if layer == DEPTH - 1 else None)
    return x.reshape(bn, t, d)


def _tables(t5_table, rpb_c):
    a_biases = []
    for g, (_, d) in enumerate(DILATIONS):
        cols = t5_table[:, g * A_HEADS_PER_GROUP:(g + 1) * A_HEADS_PER_GROUP]
        a_biases.append(_band_bias(cols, A_SUB, A_HALF, d))
    b_bias = _band_bias(t5_table[:, A_HEADS:], B_SUB, B_HALF_WINDOW, 1)
    b_bias = b_bias.reshape(B_KV_HEADS, B_REP * B_SUB, B_KEYS)
    c_biases = [_na_bias(rpb_c[j]) for j in range(rpb_c.shape[0])]
    perms = jnp.asarray(np.stack([_deinterleave_perm(d) for _, d in DILATIONS[1:]]), bf16)
    return a_biases, b_bias, c_biases, perms


def kernel(x_prompt, x_sample, w_in_ab, w_out_ab, sink_b, w_in_c, w_out_c, rpb_c, t5_table, norm_mix, norm_ffn, w_gate, w_up, w_down, norm_final):
    a_biases, b_bias, c_biases, perms = _tables(t5_table, rpb_c)
    weights = (w_in_ab.astype(bf16), w_out_ab.astype(bf16), sink_b, w_in_c.astype(bf16), w_out_c.astype(bf16),
               a_biases, b_bias, c_biases, perms, norm_mix, norm_ffn,
               w_gate.astype(bf16), w_up.astype(bf16), w_down.astype(bf16), norm_final)
    return (_trunk(x_prompt, *weights), _trunk(x_sample, *weights))
```

```python
import functools

import numpy as np
import jax
import jax.numpy as jnp
from jax import lax
from jax.experimental import pallas as pl
from jax.experimental.pallas import tpu as pltpu

D_MODEL = 2048
DEPTH = 4
HEAD_DIM = 128
DILATIONS = ((128, 1), (512, 4), (2048, 16))
A_HEADS_PER_GROUP = 4
A_HEADS = A_HEADS_PER_GROUP * len(DILATIONS)
B_Q_HEADS = 8
B_KV_HEADS = 2
B_REP = B_Q_HEADS // B_KV_HEADS
B_HALF_WINDOW = 128
C_HEADS = 16
GRID_W = 64
NA_ROWS = 8
NA_COLS = 16
NUM_BUCKETS = 32
MAX_DISTANCE = 1024
RMS_EPS = 1e-6
A_GROUP_IN = 3 * A_HEADS_PER_GROUP * HEAD_DIM
A_IN = len(DILATIONS) * A_GROUP_IN
B_IN = (B_Q_HEADS + 2 * B_KV_HEADS) * HEAD_DIM
AB_IN = A_IN + B_IN
N_AB_SLABS = AB_IN // A_GROUP_IN
A_OUT = A_HEADS_PER_GROUP * HEAD_DIM
A_OUT_W = A_OUT + HEAD_DIM
B_OUT = B_Q_HEADS * HEAD_DIM
AB_OUT = A_OUT + B_OUT
C_IN = 3 * C_HEADS * HEAD_DIM
C_OUT = C_HEADS * HEAD_DIM
SCALE = HEAD_DIM ** -0.5
NEG_INF = -1e30

A_HALF = 64
A_CHUNK = 1024
A_SUB = 128
PERM_PIECE = 256
NORM_BLOCK = 256
B_SUB = 128
B_KEYS = B_SUB + 2 * B_HALF_WINDOW
C_QROWS = 4
C_QTOK = C_QROWS * GRID_W
C_KBLKS = 3
C_HG = 8
LOG2E = 1.4426950408889634

VMEM_LIMIT = 52 * 1024 * 1024
FFN_VMEM_LIMIT = 58 * 1024 * 1024

f32 = jnp.float32
bf16 = jnp.bfloat16


def _cparams(sem, vmem_limit=VMEM_LIMIT):
    return pltpu.CompilerParams(dimension_semantics=sem, vmem_limit_bytes=vmem_limit)


def _resident(shape, index_map):
    return pl.BlockSpec(shape, index_map, pipeline_mode=pl.Buffered(1))


def _rms_rows(x_ref, g_ref, dst_ref, rows, chunk=64):
    def body(c, carry):
        r = pl.multiple_of(c * chunk, chunk)
        x = x_ref[pl.ds(r, chunk), :]
        ms = jnp.mean(x * x, axis=-1, keepdims=True)
        dst_ref[pl.ds(r, chunk), :] = ((x * lax.rsqrt(ms + RMS_EPS)) * g_ref[...]).astype(dst_ref.dtype)
        return carry
    lax.fori_loop(0, rows // chunk, body, 0, unroll=2)


def _rms_block(x_ref, g_ref, dst_ref, r0, rows, piece=64):
    for r in range(r0, r0 + rows, piece):
        x = x_ref[r:r + piece, :]
        ms = jnp.mean(x * x, axis=-1, keepdims=True)
        dst_ref[r:r + piece, :] = ((x * lax.rsqrt(ms + RMS_EPS)) * g_ref[...]).astype(dst_ref.dtype)


def _norm_matmul_kernel(x_ref, g_ref, w_ref, o_ref, h_ref, *, tm):
    j = pl.program_id(1)

    @pl.when(j == 0)
    def _():
        for r0 in range(0, tm, NORM_BLOCK):
            _rms_block(x_ref, g_ref, h_ref, r0, NORM_BLOCK)
            o_ref[r0:r0 + NORM_BLOCK, :] = jnp.dot(h_ref[r0:r0 + NORM_BLOCK, :], w_ref[...],
                                                   preferred_element_type=f32).astype(o_ref.dtype)

    @pl.when(j > 0)
    def _():
        o_ref[...] = jnp.dot(h_ref[...], w_ref[...], preferred_element_type=f32).astype(o_ref.dtype)


def _norm_matmul(x, g, w, li, *, tm=1024, tn=1536):
    m, d = x.shape
    n = w.shape[2]
    return pl.pallas_call(
        functools.partial(_norm_matmul_kernel, tm=tm),
        grid=(m // tm, n // tn),
        in_specs=[pl.BlockSpec((tm, d), lambda i, j: (i, 0)),
                  _resident((1, d), lambda i, j: (0, 0)),
                  pl.BlockSpec((None, d, tn), lambda i, j: (li, 0, j))],
        out_specs=pl.BlockSpec((tm, tn), lambda i, j: (i, j)),
        out_shape=jax.ShapeDtypeStruct((m, n), bf16),
        scratch_shapes=[pltpu.VMEM((tm, d), bf16)],
        compiler_params=_cparams(("parallel", "arbitrary")),
        name="norm_proj",
    )(x, g.reshape(1, d), w)


def _deinterleave_perm(d):
    per = PERM_PIECE // d
    p = np.zeros((PERM_PIECE, PERM_PIECE), np.float32)
    for r in range(d):
        for v in range(per):
            p[r * per + v, v * d + r] = 1.0
    return p


def _norm_proj_ab_kernel(x_ref, g_ref, w_ref, perm_ref, o_ref, h_ref, *, tm, steps_per_slab):
    j = pl.program_id(1)
    slab = j // steps_per_slab

    def project(r0, rows):
        return jnp.dot(h_ref[r0:r0 + rows, :], w_ref[...], preferred_element_type=f32).astype(bf16)

    @pl.when(j == 0)
    def _():
        for r0 in range(0, tm, NORM_BLOCK):
            _rms_block(x_ref, g_ref, h_ref, r0, NORM_BLOCK)
            o_ref[0, r0:r0 + NORM_BLOCK, :] = project(r0, NORM_BLOCK)

    @pl.when(jnp.logical_and(j > 0, jnp.logical_or(slab == 0, slab == N_AB_SLABS - 1)))
    def _():
        o_ref[0] = project(0, tm)

    for g, (_, d) in enumerate(DILATIONS):
        if d == 1:
            continue

        @pl.when(slab == g)
        def _(g=g, d=d):
            t = project(0, tm)
            run = tm // d
            per = PERM_PIECE // d
            for p in range(tm // PERM_PIECE):
                piece = jnp.dot(perm_ref[g - 1], t[p * PERM_PIECE:(p + 1) * PERM_PIECE, :],
                                preferred_element_type=f32).astype(bf16)
                for r in range(d):
                    o_ref[0, r * run + p * per:r * run + (p + 1) * per, :] = piece[r * per:(r + 1) * per, :]


def _norm_proj_ab(x, g, w, li, perms, *, tn=1536):
    m, d = x.shape
    tm = A_CHUNK
    steps_per_slab = A_GROUP_IN // tn
    return pl.pallas_call(
        functools.partial(_norm_proj_ab_kernel, tm=tm, steps_per_slab=steps_per_slab),
        grid=(m // tm, AB_IN // tn),
        in_specs=[pl.BlockSpec((tm, d), lambda i, j: (i, 0)),
                  _resident((1, d), lambda i, j: (0, 0)),
                  pl.BlockSpec((None, d, tn), lambda i, j: (li, 0, j)),
                  _resident(perms.shape, lambda i, j: (0, 0, 0))],
        out_specs=pl.BlockSpec((1, tm, tn), lambda i, j: (j // steps_per_slab, i, j % steps_per_slab)),
        out_shape=jax.ShapeDtypeStruct((N_AB_SLABS, m, A_GROUP_IN), bf16),
        scratch_shapes=[pltpu.VMEM((tm, d), bf16)],
        compiler_params=_cparams(("parallel", "arbitrary")),
        name="norm_proj_ab",
    )(x, g.reshape(1, d), w, perms)


def _ffn_kernel(x_ref, g_ref, wg_ref, wu_ref, wd_ref, *rest, tm, nf, final):
    if final:
        gf_ref, o_ref, h_ref = rest
    else:
        o_ref, h_ref = rest
    f = pl.program_id(1)

    def swiglu_rows(r0, rows):
        h = h_ref[r0:r0 + rows, :]
        gate = jnp.dot(h, wg_ref[...], preferred_element_type=f32)
        up = jnp.dot(h, wu_ref[...], preferred_element_type=f32)
        act = (gate * jax.nn.sigmoid(gate) * up).astype(bf16)
        return jnp.dot(act, wd_ref[...], preferred_element_type=f32)

    @pl.when(f == 0)
    def _():
        for r0 in range(0, tm, NORM_BLOCK):
            _rms_block(x_ref, g_ref, h_ref, r0, NORM_BLOCK)
            o_ref[r0:r0 + NORM_BLOCK, :] = x_ref[r0:r0 + NORM_BLOCK, :] + swiglu_rows(r0, NORM_BLOCK)

    @pl.when(f > 0)
    def _():
        o_ref[...] += swiglu_rows(0, tm)

    if final:
        @pl.when(f == nf - 1)
        def _():
            _rms_rows(o_ref, gf_ref, o_ref, tm)


def _ffn(x, g, wg, wu, wd, li, g_final=None, *, tm=1024, tf=512):
    m, d = x.shape
    dff = wg.shape[2]
    nf = dff // tf
    final = g_final is not None
    in_specs = [pl.BlockSpec((tm, d), lambda i, f: (i, 0)),
                _resident((1, d), lambda i, f: (0, 0)),
                pl.BlockSpec((None, d, tf), lambda i, f: (li, 0, f)),
                pl.BlockSpec((None, d, tf), lambda i, f: (li, 0, f)),
                pl.BlockSpec((None, tf, d), lambda i, f: (li, f, 0))]
    args = [x, g.reshape(1, d), wg, wu, wd]
    if final:
        in_specs.append(_resident((1, d), lambda i, f: (0, 0)))
        args.append(g_final.reshape(1, d))
    return pl.pallas_call(
        functools.partial(_ffn_kernel, tm=tm, nf=nf, final=final),
        grid=(m // tm, nf),
        in_specs=in_specs,
        out_specs=pl.BlockSpec((tm, d), lambda i, f: (i, 0)),
        out_shape=jax.ShapeDtypeStruct((m, d), f32),
        scratch_shapes=[pltpu.VMEM((tm, d), bf16)],
        compiler_params=_cparams(("parallel", "arbitrary"), FFN_VMEM_LIMIT),
        name="ffn",
    )(*args)


def _t5_bucket(rel):
    nb = NUM_BUCKETS // 2
    max_exact = nb // 2
    ret = (rel > 0).astype(np.int32) * nb
    n = np.abs(rel)
    large = max_exact + (np.log(np.maximum(n, 1) / max_exact) / np.log(MAX_DISTANCE / max_exact)
                         * (nb - max_exact)).astype(np.int32)
    large = np.minimum(large, nb - 1)
    return (ret + np.where(n < max_exact, n, large)).astype(np.int32)


def _band_bias(table_cols, n_q, half, dilation):
    n_k = n_q + 2 * half
    w = n_q + n_k
    rel = np.arange(w) - (n_q - 1) - half
    diag = table_cols[_t5_bucket(rel * dilation)].astype(f32)
    diag = jnp.where((np.abs(rel) <= half)[:, None], diag, NEG_INF).T
    flat = jnp.tile(diag, (1, n_q + 1))[:, n_q - 1:n_q - 1 + n_q * (w - 1)]
    return flat.reshape(-1, n_q, w - 1)[:, :, :n_k]


def _a_geometry(d):
    run = A_CHUNK // d
    n_span = max(1, A_SUB // run)
    seg = n_span * run
    stride = seg + 2 * A_HALF
    return run, n_span, seg, stride


def _attn_a_kernel(q_ref, kp_ref, kc_ref, kn_ref, vp_ref, vc_ref, vn_ref, bias_ref, o_ref,
                   kbuf, vbuf, *, d, seq_len):
    run, n_span, seg, stride = _a_geometry(d)
    piece = min(run, A_SUB)
    n_keys = A_SUB + 2 * A_HALF
    step = pl.program_id(1)
    for buf, p_ref, c_ref, n_ref in ((kbuf, kp_ref, kc_ref, kn_ref), (vbuf, vp_ref, vc_ref, vn_ref)):
        for r in range(d):
            base = r * stride
            buf[base:base + A_HALF] = p_ref[(r + 1) * run - A_HALF:(r + 1) * run]
            for sp in range(n_span):
                lo = base + A_HALF + sp * run
                buf[lo:lo + run] = c_ref[sp * A_CHUNK + r * run:sp * A_CHUNK + (r + 1) * run]
            buf[base + A_HALF + seg:base + stride] = n_ref[r * run:r * run + A_HALF]
    lane = lax.broadcasted_iota(jnp.int32, (A_SUB, HEAD_DIM), 1)

    for r in range(d):
        for sb in range(seg // A_SUB):
            pos0 = sb * A_SUB
            rows = [(p // run) * A_CHUNK + r * run + p % run for p in range(pos0, pos0 + A_SUB, piece)]
            k0 = r * stride + pos0
            kpos = step * seg + (pos0 - A_HALF) + lax.broadcasted_iota(jnp.int32, (1, n_keys), 1)
            valid = jnp.where(kpos >= 0, kpos, seq_len) < seq_len
            lse_tile = jnp.zeros((A_SUB, HEAD_DIM), f32)
            for h in range(A_HEADS_PER_GROUP):
                c0 = h * HEAD_DIM
                q = jnp.concatenate([q_ref[q0:q0 + piece, c0:c0 + HEAD_DIM] for q0 in rows], axis=0)
                k = kbuf[k0:k0 + n_keys, c0:c0 + HEAD_DIM]
                v = vbuf[k0:k0 + n_keys, c0:c0 + HEAD_DIM]
                s = lax.dot_general(q, k, (((1,), (1,)), ((), ())), preferred_element_type=f32)
                s = s * SCALE + bias_ref[h]
                s = jnp.where(valid, s, NEG_INF)
                m = jnp.max(s, axis=-1, keepdims=True)
                p = jnp.exp(s - m)
                den = jnp.sum(p, axis=-1, keepdims=True)
                o = jnp.dot(p.astype(bf16), v, preferred_element_type=f32) / den
                for i, q0 in enumerate(rows):
                    o_ref[0, q0:q0 + piece, c0:c0 + HEAD_DIM] = o[i * piece:(i + 1) * piece]
                lse_tile = jnp.where(lane == h, m + jnp.log(den), lse_tile)
            for i, q0 in enumerate(rows):
                o_ref[0, q0:q0 + piece, A_OUT:] = lse_tile[i * piece:(i + 1) * piece]


def _attn_a_group(proj, bias, g, d, bn, t):
    run, n_span, seg, stride = _a_geometry(d)
    nc = t // A_CHUNK
    pv = proj.reshape(N_AB_SLABS, bn, t, A_GROUP_IN)

    def cur(col):
        return pl.BlockSpec((None, None, n_span * A_CHUNK, A_OUT), lambda b, c: (g, b, c, col))

    def prev(col):
        return pl.BlockSpec((None, None, A_CHUNK, A_OUT), lambda b, c: (g, b, jnp.maximum(c * n_span - 1, 0), col))

    def nxt(col):
        return pl.BlockSpec((None, None, A_CHUNK, A_OUT),
                            lambda b, c: (g, b, jnp.minimum((c + 1) * n_span, nc - 1), col))

    return pl.pallas_call(
        functools.partial(_attn_a_kernel, d=d, seq_len=t // d),
        grid=(bn, nc // n_span),
        in_specs=[cur(0), prev(1), cur(1), nxt(1), prev(2), cur(2), nxt(2),
                  _resident((A_HEADS_PER_GROUP, A_SUB, A_SUB + 2 * A_HALF), lambda b, c: (0, 0, 0))],
        out_specs=pl.BlockSpec((1, n_span * A_CHUNK, A_OUT_W), lambda b, c: (b, c, 0)),
        out_shape=jax.ShapeDtypeStruct((bn, t, A_OUT_W), f32),
        scratch_shapes=[pltpu.VMEM((d * stride, A_OUT), bf16),
                        pltpu.VMEM((d * stride, A_OUT), bf16)],
        compiler_params=_cparams(("parallel", "arbitrary")),
        name=f"attn_a{g}",
    )(pv, pv, pv, pv, pv, pv, pv, bias)


def _attn_b_kernel(sink_ref, cur_ref, hp_ref, hn_ref, bias_ref, o_ref, kbuf, vbuf, *, tq, seq_len):
    i = pl.program_id(1)
    kw = B_KV_HEADS * HEAD_DIM
    k_off = B_OUT
    v_off = B_OUT + kw
    hw = B_HALF_WINDOW
    kbuf[0:hw] = hp_ref[:, 0:kw]
    kbuf[hw:hw + tq] = cur_ref[:, k_off:k_off + kw]
    kbuf[hw + tq:] = hn_ref[:, 0:kw]
    vbuf[0:hw] = hp_ref[:, kw:2 * kw]
    vbuf[hw:hw + tq] = cur_ref[:, v_off:v_off + kw]
    vbuf[hw + tq:] = hn_ref[:, kw:2 * kw]
    for sb in range(tq // B_SUB):
        r0 = sb * B_SUB
        kpos = i * tq + (r0 - hw) + lax.broadcasted_iota(jnp.int32, (1, B_KEYS), 1)
        valid = jnp.where(kpos >= 0, kpos, seq_len) < seq_len
        for g in range(B_KV_HEADS):
            qs = jnp.concatenate(
                [cur_ref[r0:r0 + B_SUB, (g * B_REP + hh) * HEAD_DIM:(g * B_REP + hh + 1) * HEAD_DIM]
                 for hh in range(B_REP)], axis=0)
            k = kbuf[r0:r0 + B_KEYS, g * HEAD_DIM:(g + 1) * HEAD_DIM]
            v = vbuf[r0:r0 + B_KEYS, g * HEAD_DIM:(g + 1) * HEAD_DIM]
            s_all = lax.dot_general(qs, k, (((1,), (1,)), ((), ())), preferred_element_type=f32)
            s_all = s_all * SCALE + bias_ref[g]
            s_all = jnp.where(valid, s_all, NEG_INF)
            ps, dens = [], []
            for hh in range(B_REP):
                s = s_all[hh * B_SUB:(hh + 1) * B_SUB]
                sk = sink_ref[g * B_REP + hh]
                m = jnp.maximum(jnp.max(s, axis=-1, keepdims=True), sk)
                p = jnp.exp(s - m)
                dens.append(jnp.sum(p, axis=-1, keepdims=True) + jnp.exp(sk - m))
                ps.append(p.astype(bf16))
            o_all = jnp.dot(jnp.concatenate(ps, axis=0), v, preferred_element_type=f32)
            for hh in range(B_REP):
                c0 = (g * B_REP + hh) * HEAD_DIM
                o = o_all[hh * B_SUB:(hh + 1) * B_SUB] / dens[hh]
                o_ref[0, r0:r0 + B_SUB, c0:c0 + HEAD_DIM] = o.astype(o_ref.dtype)


def _attn_b(proj, bias, sink, bn, t):
    tq = 256
    nblk = t // tq
    hb = tq // B_HALF_WINDOW
    n_halo = t // B_HALF_WINDOW
    kvw = 2 * B_KV_HEADS * HEAD_DIM
    slab = N_AB_SLABS - 1
    pv = proj.reshape(N_AB_SLABS, bn, t, B_IN)
    out = pl.pallas_call(
        functools.partial(_attn_b_kernel, tq=tq, seq_len=t),
        grid=(bn, nblk),
        in_specs=[pl.BlockSpec(memory_space=pltpu.SMEM),
                  pl.BlockSpec((None, None, tq, B_IN), lambda b, i: (slab, b, i, 0)),
                  pl.BlockSpec((None, None, B_HALF_WINDOW, kvw),
                               lambda b, i: (slab, b, jnp.maximum(i * hb - 1, 0), B_IN // kvw - 1)),
                  pl.BlockSpec((None, None, B_HALF_WINDOW, kvw),
                               lambda b, i: (slab, b, jnp.minimum((i + 1) * hb, n_halo - 1), B_IN // kvw - 1)),
                  _resident((B_KV_HEADS, B_REP * B_SUB, B_KEYS), lambda b, i: (0, 0, 0))],
        out_specs=pl.BlockSpec((1, tq, B_OUT), lambda b, i: (b, i, 0)),
        out_shape=jax.ShapeDtypeStruct((bn, t, B_OUT), bf16),
        scratch_shapes=[pltpu.VMEM((tq + 2 * B_HALF_WINDOW, B_KV_HEADS * HEAD_DIM), bf16),
                        pltpu.VMEM((tq + 2 * B_HALF_WINDOW, B_KV_HEADS * HEAD_DIM), bf16)],
        compiler_params=_cparams(("parallel", "arbitrary")),
        name="attn_b",
    )(sink, pv, pv, pv, bias)
    return out.reshape(bn * t, B_OUT)


def _out_ab_kernel(x_ref, a0_ref, a1_ref, a2_ref, ob_ref, w_ref, o_ref, ilv_ref, mix_ref, *, tm):
    n_slabs = A_OUT_W // HEAD_DIM
    for gi, (a_ref, (_, d)) in enumerate(((a1_ref, DILATIONS[1]), (a2_ref, DILATIONS[2]))):
        per = tm // d
        for r in range(d):
            for s in range(n_slabs):
                ilv_ref[gi, s, pl.ds(r, per, stride=d), :] = a_ref[0, r, :, s * HEAD_DIM:(s + 1) * HEAD_DIM]
    for h in range(A_HEADS_PER_GROUP):
        c0 = h * HEAD_DIM
        outs = [a0_ref[:, c0:c0 + HEAD_DIM], ilv_ref[0, h], ilv_ref[1, h]]
        lses = [a0_ref[:, A_OUT + h:A_OUT + h + 1],
                ilv_ref[0, n_slabs - 1, :, h:h + 1],
                ilv_ref[1, n_slabs - 1, :, h:h + 1]]
        mx = jnp.maximum(jnp.maximum(lses[0], lses[1]), lses[2])
        es = [jnp.exp(l - mx) for l in lses]
        tot = es[0] + es[1] + es[2]
        acc = (es[0] / tot) * outs[0]
        acc += (es[1] / tot) * outs[1]
        acc += (es[2] / tot) * outs[2]
        mix_ref[:, c0:c0 + HEAD_DIM] = acc.astype(bf16)
    y = jnp.dot(ob_ref[...], w_ref[A_OUT:, :], preferred_element_type=f32)
    y += jnp.dot(mix_ref[...], w_ref[:A_OUT, :], preferred_element_type=f32)
    o_ref[...] = x_ref[...] + y


def _out_ab(x, a_outs, o_b, w, li, *, tm=512):
    m, d = x.shape
    row = lambda i: (i, 0)
    tiles = A_CHUNK // tm
    a0 = a_outs[0].reshape(m, A_OUT_W)
    dil = [dd for _, dd in DILATIONS]
    a1 = a_outs[1].reshape(m // A_CHUNK, dil[1], A_CHUNK // dil[1], A_OUT_W)
    a2 = a_outs[2].reshape(m // A_CHUNK, dil[2], A_CHUNK // dil[2], A_OUT_W)

    def deint(dd):
        return pl.BlockSpec((1, dd, tm // dd, A_OUT_W), lambda i: (i // tiles, 0, i % tiles, 0))

    return pl.pallas_call(
        functools.partial(_out_ab_kernel, tm=tm),
        grid=(m // tm,),
        in_specs=[pl.BlockSpec((tm, d), row),
                  pl.BlockSpec((tm, A_OUT_W), row),
                  deint(dil[1]),
                  deint(dil[2]),
                  pl.BlockSpec((tm, B_OUT), row),
                  _resident((None, AB_OUT, d), lambda i: (li, 0, 0))],
        out_specs=pl.BlockSpec((tm, d), row),
        out_shape=jax.ShapeDtypeStruct((m, d), f32),
        scratch_shapes=[pltpu.VMEM((2, A_OUT_W // HEAD_DIM, tm, HEAD_DIM), f32),
                        pltpu.VMEM((tm, A_OUT), bf16)],
        compiler_params=_cparams(("parallel",)),
        name="out_ab",
    )(x, a0, a1, a2, o_b, w)


def _na_bias(rpb):
    n_heads = rpb.shape[0]
    n_krows = C_KBLKS * C_QROWS
    c = np.arange(GRID_W)
    cs = np.clip(c - NA_COLS // 2, 0, GRID_W - NA_COLS)
    col_ok = (c[None, :] >= cs[:, None]) & (c[None, :] < cs[:, None] + NA_COLS)
    dc = np.clip(c[None, :] - c[:, None] + NA_COLS - 1, 0, 2 * NA_COLS - 2)
    onehot = (dc[None] == np.arange(2 * NA_COLS - 1)[:, None, None]).astype(np.float32)
    tiles = jnp.einsum('hdc,cqk->hqdk', rpb.astype(f32), onehot, precision=lax.Precision.HIGHEST)
    tiles = jnp.where(col_ok[None, :, None, :], tiles, NEG_INF)
    out = []
    for case in range(3):
        rows = []
        for qi in range(C_QROWS):
            first = (0, qi, C_QROWS)[case]
            dr0 = first - qi - case * C_QROWS + NA_ROWS - 1
            band = tiles[:, :, dr0:dr0 + NA_ROWS, :]
            rows.append(jnp.pad(band, ((0, 0), (0, 0), (first, n_krows - NA_ROWS - first), (0, 0)),
                                constant_values=NEG_INF))
        out.append(jnp.stack(rows, axis=1).reshape(n_heads, C_QTOK, n_krows * GRID_W))
    return jnp.stack(out) * LOG2E


def _attn_c_kernel(q_ref, k0_ref, k1_ref, k2_ref, v0_ref, v1_ref, v2_ref, bias_ref, o_ref):
    k_refs = (k0_ref, k1_ref, k2_ref)
    v_refs = (v0_ref, v1_ref, v2_ref)
    for h in range(C_HG):
        c0 = h * HEAD_DIM
        q = q_ref[0, :, c0:c0 + HEAD_DIM]
        ss = []
        for j in range(C_KBLKS):
            s = lax.dot_general(q, k_refs[j][0, :, c0:c0 + HEAD_DIM], (((1,), (1,)), ((), ())),
                                preferred_element_type=f32)
            ss.append(s * (SCALE * LOG2E) + bias_ref[0, h, :, j * C_QTOK:(j + 1) * C_QTOK])
        m = jnp.max(ss[0], axis=-1, keepdims=True)
        for j in range(1, C_KBLKS):
            m = jnp.maximum(m, jnp.max(ss[j], axis=-1, keepdims=True))
        den = None
        o = None
        for j in range(C_KBLKS):
            p = jnp.exp2(ss[j] - m)
            dj = jnp.sum(p, axis=-1, keepdims=True)
            oj = jnp.dot(p.astype(bf16), v_refs[j][0, :, c0:c0 + HEAD_DIM], preferred_element_type=f32)
            den = dj if den is None else den + dj
            o = oj if o is None else o + oj
        o_ref[0, :, c0:c0 + HEAD_DIM] = (o / den).astype(o_ref.dtype)


def _attn_c(qkv, bias):
    bn, t, _ = qkv.shape
    nblk = t // C_QTOK
    hgw = C_HG * HEAD_DIM
    n_hg = C_HEADS // C_HG
    k_cb = C_OUT // hgw
    v_cb = 2 * C_OUT // hgw

    def kv_spec(cb, j):
        return pl.BlockSpec((1, C_QTOK, hgw),
                            lambda g, b, i: (b, jnp.clip(i - 1, 0, nblk - C_KBLKS) + j, cb + g))

    def case(i):
        return jnp.where(i == 0, 0, jnp.where(i == nblk - 1, 2, 1))

    out = pl.pallas_call(
        _attn_c_kernel,
        grid=(n_hg, bn, nblk),
        in_specs=[pl.BlockSpec((1, C_QTOK, hgw), lambda g, b, i: (b, i, g)),
                  kv_spec(k_cb, 0), kv_spec(k_cb, 1), kv_spec(k_cb, 2),
                  kv_spec(v_cb, 0), kv_spec(v_cb, 1), kv_spec(v_cb, 2),
                  pl.BlockSpec((1, C_HG, C_QTOK, C_KBLKS * C_QTOK), lambda g, b, i: (case(i), g, 0, 0))],
        out_specs=pl.BlockSpec((1, C_QTOK, hgw), lambda g, b, i: (b, i, g)),
        out_shape=jax.ShapeDtypeStruct((bn, t, C_OUT), bf16),
        compiler_params=_cparams(("parallel", "parallel", "arbitrary")),
        name="attn_c",
    )(qkv, qkv, qkv, qkv, qkv, qkv, qkv, bias)
    return out.reshape(bn * t, C_OUT)


def _out_proj_kernel(x_ref, a_ref, w_ref, o_ref):
    o_ref[...] = x_ref[...] + jnp.dot(a_ref[...], w_ref[...], preferred_element_type=f32)


def _out_proj(x, a, w, li, *, tm=512):
    m, d = x.shape
    k = a.shape[1]
    row = lambda i: (i, 0)
    return pl.pallas_call(
        _out_proj_kernel,
        grid=(m // tm,),
        in_specs=[pl.BlockSpec((tm, d), row),
                  pl.BlockSpec((tm, k), row),
                  _resident((None, k, d), lambda i: (li, 0, 0))],
        out_specs=pl.BlockSpec((tm, d), row),
        out_shape=jax.ShapeDtypeStruct((m, d), f32),
        compiler_params=_cparams(("parallel",)),
        name="out_c",
    )(x, a, w)


def _mixer_ab(x, bn, t, g_norm, w_in, w_out, li, sink, a_biases, b_bias, perms):
    proj = _norm_proj_ab(x, g_norm, w_in, li, perms)
    a_outs = [_attn_a_group(proj, a_biases[g], g, d, bn, t) for g, (_, d) in enumerate(DILATIONS)]
    o_b = _attn_b(proj, b_bias, sink, bn, t)
    return _out_ab(x, a_outs, o_b, w_out, li)


def _mixer_c(x, bn, t, g_norm, w_in, w_out, li, c_bias):
    qkv = _norm_matmul(x, g_norm, w_in, li).reshape(bn, t, C_IN)
    return _out_proj(x, _attn_c(qkv, c_bias), w_out, li)


def _trunk(x3, w_in_ab, w_out_ab, sink_b, w_in_c, w_out_c, a_biases, b_bias, c_biases, perms,
           norm_mix, norm_ffn, w_gate, w_up, w_down, norm_final):
    bn, t, d = x3.shape
    x = x3.reshape(bn * t, d)
    for layer in range(DEPTH):
        j = layer // 2
        if layer % 2 == 0:
            x = _mixer_ab(x, bn, t, norm_mix[layer], w_in_ab, w_out_ab, j, sink_b[j], a_biases, b_bias, perms)
        else:
            x = _mixer_c(x, bn, t, norm_mix[layer], w_in_c, w_out_c, j, c_biases[j])
        x = _ffn(x, norm_ffn[layer], w_gate, w_up, w_down, layer,
                 norm_final if layer == DEPTH - 1 else None)
    return x.reshape(bn, t, d)


def _tables(t5_table, rpb_c):
    a_biases = []
    for g, (_, d) in enumerate(DILATIONS):
        cols = t5_table[:, g * A_HEADS_PER_GROUP:(g + 1) * A_HEADS_PER_GROUP]
        a_biases.append(_band_bias(cols, A_SUB, A_HALF, d))
    b_bias = _band_bias(t5_table[:, A_HEADS:], B_SUB, B_HALF_WINDOW, 1)
    b_bias = b_bias.reshape(B_KV_HEADS, B_REP * B_SUB, B_KEYS)
    c_biases = [_na_bias(rpb_c[j]) for j in range(rpb_c.shape[0])]
    perms = jnp.asarray(np.stack([_deinterleave_perm(d) for _, d in DILATIONS[1:]]), bf16)
    return a_biases, b_bias, c_biases, perms


def kernel(x_prompt, x_sample, w_in_ab, w_out_ab, sink_b, w_in_c, w_out_c, rpb_c, t5_table, norm_mix, norm_ffn, w_gate, w_up, w_down, norm_final):
    a_biases, b_bias, c_biases, perms = _tables(t5_table, rpb_c)
    weights = (w_in_ab.astype(bf16), w_out_ab.astype(bf16), sink_b, w_in_c.astype(bf16), w_out_c.astype(bf16),
               a_biases, b_bias, c_biases, perms, norm_mix, norm_ffn,
               w_gate.astype(bf16), w_up.astype(bf16), w_down.astype(bf16), norm_final)
    return (_trunk(x_prompt, *weights), _trunk(x_sample, *weights))
```

```python
import functools

import numpy as np
import jax
import jax.numpy as jnp
from jax import lax
from jax.experimental import pallas as pl
from jax.experimental.pallas import tpu as pltpu

D_MODEL = 2048
DEPTH = 4
HEAD_DIM = 128
DILATIONS = ((128, 1), (512, 4), (2048, 16))
A_HEADS_PER_GROUP = 4
A_HEADS = A_HEADS_PER_GROUP * len(DILATIONS)
B_Q_HEADS = 8
B_KV_HEADS = 2
B_REP = B_Q_HEADS // B_KV_HEADS
B_HALF_WINDOW = 128
C_HEADS = 16
GRID_W = 64
NA_ROWS = 8
NA_COLS = 16
NUM_BUCKETS = 32
MAX_DISTANCE = 1024
RMS_EPS = 1e-6
A_GROUP_IN = 3 * A_HEADS_PER_GROUP * HEAD_DIM
A_IN = len(DILATIONS) * A_GROUP_IN
B_IN = (B_Q_HEADS + 2 * B_KV_HEADS) * HEAD_DIM
AB_IN = A_IN + B_IN
N_AB_SLABS = AB_IN // A_GROUP_IN
A_OUT = A_HEADS_PER_GROUP * HEAD_DIM
A_OUT_W = A_OUT + HEAD_DIM
B_OUT = B_Q_HEADS * HEAD_DIM
AB_OUT = A_OUT + B_OUT
C_IN = 3 * C_HEADS * HEAD_DIM
C_OUT = C_HEADS * HEAD_DIM
SCALE = HEAD_DIM ** -0.5
NEG_INF = -1e30

A_HALF = 64
A_CHUNK = 1024
A_SUB = 128
PERM_PIECE = 256
NORM_BLOCK = 256
B_SUB = 128
B_KEYS = B_SUB + 2 * B_HALF_WINDOW
C_QROWS = 4
C_QTOK = C_QROWS * GRID_W
C_KBLKS = 3
C_HG = 8
LOG2E = 1.4426950408889634

VMEM_LIMIT = 52 * 1024 * 1024
FFN_VMEM_LIMIT = 58 * 1024 * 1024

f32 = jnp.float32
bf16 = jnp.bfloat16


def _cparams(sem, vmem_limit=VMEM_LIMIT):
    return pltpu.CompilerParams(dimension_semantics=sem, vmem_limit_bytes=vmem_limit)


def _resident(shape, index_map):
    return pl.BlockSpec(shape, index_map, pipeline_mode=pl.Buffered(1))


def _rms_rows(x_ref, g_ref, dst_ref, rows, chunk=64):
    def body(c, carry):
        r = pl.multiple_of(c * chunk, chunk)
        x = x_ref[pl.ds(r, chunk), :]
        ms = jnp.mean(x * x, axis=-1, keepdims=True)
        dst_ref[pl.ds(r, chunk), :] = ((x * lax.rsqrt(ms + RMS_EPS)) * g_ref[...]).astype(dst_ref.dtype)
        return carry
    lax.fori_loop(0, rows // chunk, body, 0, unroll=2)


def _rms_block(x_ref, g_ref, dst_ref, r0, rows, piece=64, src0=None):
    src0 = r0 if src0 is None else src0
    for k in range(0, rows, piece):
        x = x_ref[src0 + k:src0 + k + piece, :]
        ms = jnp.mean(x * x, axis=-1, keepdims=True)
        dst_ref[r0 + k:r0 + k + piece, :] = ((x * lax.rsqrt(ms + RMS_EPS)) * g_ref[...]).astype(dst_ref.dtype)


def _norm_matmul_kernel(x_ref, g_ref, w_ref, o_ref, h_ref, *, tm):
    j = pl.program_id(1)

    @pl.when(j == 0)
    def _():
        for r0 in range(0, tm, NORM_BLOCK):
            _rms_block(x_ref, g_ref, h_ref, r0, NORM_BLOCK)
            o_ref[r0:r0 + NORM_BLOCK, :] = jnp.dot(h_ref[r0:r0 + NORM_BLOCK, :], w_ref[...],
                                                   preferred_element_type=f32).astype(o_ref.dtype)

    @pl.when(j > 0)
    def _():
        o_ref[...] = jnp.dot(h_ref[...], w_ref[...], preferred_element_type=f32).astype(o_ref.dtype)


def _norm_matmul(x, g, w, li, *, tm=1024, tn=1536):
    m, d = x.shape
    n = w.shape[2]
    return pl.pallas_call(
        functools.partial(_norm_matmul_kernel, tm=tm),
        grid=(m // tm, n // tn),
        in_specs=[pl.BlockSpec((tm, d), lambda i, j: (i, 0)),
                  _resident((1, d), lambda i, j: (0, 0)),
                  pl.BlockSpec((None, d, tn), lambda i, j: (li, 0, j))],
        out_specs=pl.BlockSpec((tm, tn), lambda i, j: (i, j)),
        out_shape=jax.ShapeDtypeStruct((m, n), bf16),
        scratch_shapes=[pltpu.VMEM((tm, d), bf16)],
        compiler_params=_cparams(("parallel", "arbitrary")),
        name="norm_proj",
    )(x, g.reshape(1, d), w)


def _deinterleave_perm(d):
    per = PERM_PIECE // d
    p = np.zeros((PERM_PIECE, PERM_PIECE), np.float32)
    for r in range(d):
        for v in range(per):
            p[r * per + v, v * d + r] = 1.0
    return p


def _norm_proj_ab_kernel(x_ref, g_ref, w_ref, perm_ref, o_ref, h_ref, *, tm, steps_per_slab):
    j = pl.program_id(1)
    slab = j // steps_per_slab

    def project(r0, rows):
        return jnp.dot(h_ref[r0:r0 + rows, :], w_ref[...], preferred_element_type=f32).astype(bf16)

    @pl.when(j == 0)
    def _():
        for r0 in range(0, tm, NORM_BLOCK):
            _rms_block(x_ref, g_ref, h_ref, r0, NORM_BLOCK)
            o_ref[0, r0:r0 + NORM_BLOCK, :] = project(r0, NORM_BLOCK)

    @pl.when(jnp.logical_and(j > 0, jnp.logical_or(slab == 0, slab == N_AB_SLABS - 1)))
    def _():
        o_ref[0] = project(0, tm)

    for g, (_, d) in enumerate(DILATIONS):
        if d == 1:
            continue

        @pl.when(slab == g)
        def _(g=g, d=d):
            t = project(0, tm)
            run = tm // d
            per = PERM_PIECE // d
            for p in range(tm // PERM_PIECE):
                piece = jnp.dot(perm_ref[g - 1], t[p * PERM_PIECE:(p + 1) * PERM_PIECE, :],
                                preferred_element_type=f32).astype(bf16)
                for r in range(d):
                    o_ref[0, r * run + p * per:r * run + (p + 1) * per, :] = piece[r * per:(r + 1) * per, :]


def _norm_proj_ab(x, g, w, li, perms, *, tn=1536):
    m, d = x.shape
    tm = A_CHUNK
    steps_per_slab = A_GROUP_IN // tn
    return pl.pallas_call(
        functools.partial(_norm_proj_ab_kernel, tm=tm, steps_per_slab=steps_per_slab),
        grid=(m // tm, AB_IN // tn),
        in_specs=[pl.BlockSpec((tm, d), lambda i, j: (i, 0)),
                  _resident((1, d), lambda i, j: (0, 0)),
                  pl.BlockSpec((None, d, tn), lambda i, j: (li, 0, j)),
                  _resident(perms.shape, lambda i, j: (0, 0, 0))],
        out_specs=pl.BlockSpec((1, tm, tn), lambda i, j: (j // steps_per_slab, i, j % steps_per_slab)),
        out_shape=jax.ShapeDtypeStruct((N_AB_SLABS, m, A_GROUP_IN), bf16),
        scratch_shapes=[pltpu.VMEM((tm, d), bf16)],
        compiler_params=_cparams(("parallel", "arbitrary")),
        name="norm_proj_ab",
    )(x, g.reshape(1, d), w, perms)


def _ffn_kernel(xt_ref, xb_ref, g_ref, wg_ref, wu_ref, wd_ref, *rest, tm, nf, final):
    if final:
        gf_ref, o_ref, h_ref = rest
    else:
        o_ref, h_ref = rest
    f = pl.program_id(1)

    def swiglu_rows(r0, rows):
        h = h_ref[r0:r0 + rows, :]
        gate = jnp.dot(h, wg_ref[...], preferred_element_type=f32)
        up = jnp.dot(h, wu_ref[...], preferred_element_type=f32)
        act = (gate * jax.nn.sigmoid(gate) * up).astype(bf16)
        return jnp.dot(act, wd_ref[...], preferred_element_type=f32)

    @pl.when(f == 0)
    def _():
        for r0 in range(0, tm, NORM_BLOCK):
            x_ref, s0 = (xt_ref, r0) if r0 < tm // 2 else (xb_ref, r0 - tm // 2)
            _rms_block(x_ref, g_ref, h_ref, r0, NORM_BLOCK, src0=s0)
            o_ref[r0:r0 + NORM_BLOCK, :] = x_ref[s0:s0 + NORM_BLOCK, :] + swiglu_rows(r0, NORM_BLOCK)

    @pl.when(f > 0)
    def _():
        o_ref[...] += swiglu_rows(0, tm)

    if final:
        @pl.when(f == nf - 1)
        def _():
            _rms_rows(o_ref, gf_ref, o_ref, tm)


def _ffn(x, g, wg, wu, wd, li, g_final=None, *, tm=1024, tf=512):
    m, d = x.shape
    dff = wg.shape[2]
    nf = dff // tf
    final = g_final is not None
    n_tiles = m // tm
    in_specs = [pl.BlockSpec((tm // 2, d), lambda i, f: (2 * i, 0)),
                pl.BlockSpec((tm // 2, d),
                             lambda i, f: (2 * jnp.minimum(i + (f == nf - 1).astype(jnp.int32), n_tiles - 1) + 1, 0)),
                _resident((1, d), lambda i, f: (0, 0)),
                pl.BlockSpec((None, d, tf), lambda i, f: (li, 0, f)),
                pl.BlockSpec((None, d, tf), lambda i, f: (li, 0, f)),
                pl.BlockSpec((None, tf, d), lambda i, f: (li, f, 0))]
    args = [x, x, g.reshape(1, d), wg, wu, wd]
    if final:
        in_specs.append(_resident((1, d), lambda i, f: (0, 0)))
        args.append(g_final.reshape(1, d))
    return pl.pallas_call(
        functools.partial(_ffn_kernel, tm=tm, nf=nf, final=final),
        grid=(m // tm, nf),
        in_specs=in_specs,
        out_specs=pl.BlockSpec((tm, d), lambda i, f: (i, 0)),
        out_shape=jax.ShapeDtypeStruct((m, d), f32),
        scratch_shapes=[pltpu.VMEM((tm, d), bf16)],
        compiler_params=_cparams(("parallel", "arbitrary"), FFN_VMEM_LIMIT),
        name="ffn",
    )(*args)


def _t5_bucket(rel):
    nb = NUM_BUCKETS // 2
    max_exact = nb // 2
    ret = (rel > 0).astype(np.int32) * nb
    n = np.abs(rel)
    large = max_exact + (np.log(np.maximum(n, 1) / max_exact) / np.log(MAX_DISTANCE / max_exact)
                         * (nb - max_exact)).astype(np.int32)
    large = np.minimum(large, nb - 1)
    return (ret + np.where(n < max_exact, n, large)).astype(np.int32)


def _band_bias(table_cols, n_q, half, dilation):
    n_k = n_q + 2 * half
    w = n_q + n_k
    rel = np.arange(w) - (n_q - 1) - half
    diag = table_cols[_t5_bucket(rel * dilation)].astype(f32)
    diag = jnp.where((np.abs(rel) <= half)[:, None], diag, NEG_INF).T
    flat = jnp.tile(diag, (1, n_q + 1))[:, n_q - 1:n_q - 1 + n_q * (w - 1)]
    return flat.reshape(-1, n_q, w - 1)[:, :, :n_k]


def _a_geometry(d):
    run = A_CHUNK // d
    n_span = max(1, A_SUB // run)
    seg = n_span * run
    stride = seg + 2 * A_HALF
    return run, n_span, seg, stride


def _attn_a_kernel(q_ref, kp_ref, kc_ref, kn_ref, vp_ref, vc_ref, vn_ref, bias_ref, o_ref,
                   kbuf, vbuf, *, d, seq_len):
    run, n_span, seg, stride = _a_geometry(d)
    piece = min(run, A_SUB)
    n_keys = A_SUB + 2 * A_HALF
    step = pl.program_id(1)
    for buf, p_ref, c_ref, n_ref in ((kbuf, kp_ref, kc_ref, kn_ref), (vbuf, vp_ref, vc_ref, vn_ref)):
        for r in range(d):
            base = r * stride
            buf[base:base + A_HALF] = p_ref[(r + 1) * run - A_HALF:(r + 1) * run]
            for sp in range(n_span):
                lo = base + A_HALF + sp * run
                buf[lo:lo + run] = c_ref[sp * A_CHUNK + r * run:sp * A_CHUNK + (r + 1) * run]
            buf[base + A_HALF + seg:base + stride] = n_ref[r * run:r * run + A_HALF]
    lane = lax.broadcasted_iota(jnp.int32, (A_SUB, HEAD_DIM), 1)

    for r in range(d):
        for sb in range(seg // A_SUB):
            pos0 = sb * A_SUB
            rows = [(p // run) * A_CHUNK + r * run + p % run for p in range(pos0, pos0 + A_SUB, piece)]
            k0 = r * stride + pos0
            kpos = step * seg + (pos0 - A_HALF) + lax.broadcasted_iota(jnp.int32, (1, n_keys), 1)
            valid = jnp.where(kpos >= 0, kpos, seq_len) < seq_len
            lse_tile = jnp.zeros((A_SUB, HEAD_DIM), f32)
            for h in range(A_HEADS_PER_GROUP):
                c0 = h * HEAD_DIM
                q = jnp.concatenate([q_ref[q0:q0 + piece, c0:c0 + HEAD_DIM] for q0 in rows], axis=0)
                k = kbuf[k0:k0 + n_keys, c0:c0 + HEAD_DIM]
                v = vbuf[k0:k0 + n_keys, c0:c0 + HEAD_DIM]
                s = lax.dot_general(q, k, (((1,), (1,)), ((), ())), preferred_element_type=f32)
                s = s * SCALE + bias_ref[h]
                s = jnp.where(valid, s, NEG_INF)
                m = jnp.max(s, axis=-1, keepdims=True)
                p = jnp.exp(s - m)
                den = jnp.sum(p, axis=-1, keepdims=True)
                o = jnp.dot(p.astype(bf16), v, preferred_element_type=f32) / den
                for i, q0 in enumerate(rows):
                    o_ref[0, q0:q0 + piece, c0:c0 + HEAD_DIM] = o[i * piece:(i + 1) * piece]
                lse_tile = jnp.where(lane == h, m + jnp.log(den), lse_tile)
            for i, q0 in enumerate(rows):
                o_ref[0, q0:q0 + piece, A_OUT:] = lse_tile[i * piece:(i + 1) * piece]


def _attn_a_group(proj, bias, g, d, bn, t):
    run, n_span, seg, stride = _a_geometry(d)
    nc = t // A_CHUNK
    pv = proj.reshape(N_AB_SLABS, bn, t, A_GROUP_IN)

    def cur(col):
        return pl.BlockSpec((None, None, n_span * A_CHUNK, A_OUT), lambda b, c: (g, b, c, col))

    def prev(col):
        return pl.BlockSpec((None, None, A_CHUNK, A_OUT), lambda b, c: (g, b, jnp.maximum(c * n_span - 1, 0), col))

    def nxt(col):
        return pl.BlockSpec((None, None, A_CHUNK, A_OUT),
                            lambda b, c: (g, b, jnp.minimum((c + 1) * n_span, nc - 1), col))

    return pl.pallas_call(
        functools.partial(_attn_a_kernel, d=d, seq_len=t // d),
        grid=(bn, nc // n_span),
        in_specs=[cur(0), prev(1), cur(1), nxt(1), prev(2), cur(2), nxt(2),
                  _resident((A_HEADS_PER_GROUP, A_SUB, A_SUB + 2 * A_HALF), lambda b, c: (0, 0, 0))],
        out_specs=pl.BlockSpec((1, n_span * A_CHUNK, A_OUT_W), lambda b, c: (b, c, 0)),
        out_shape=jax.ShapeDtypeStruct((bn, t, A_OUT_W), f32),
        scratch_shapes=[pltpu.VMEM((d * stride, A_OUT), bf16),
                        pltpu.VMEM((d * stride, A_OUT), bf16)],
        compiler_params=_cparams(("parallel", "arbitrary")),
        name=f"attn_a{g}",
    )(pv, pv, pv, pv, pv, pv, pv, bias)


def _attn_b_kernel(sink_ref, cur_ref, hp_ref, hn_ref, bias_ref, o_ref, kbuf, vbuf, *, tq, seq_len):
    i = pl.program_id(1)
    kw = B_KV_HEADS * HEAD_DIM
    k_off = B_OUT
    v_off = B_OUT + kw
    hw = B_HALF_WINDOW
    kbuf[0:hw] = hp_ref[:, 0:kw]
    kbuf[hw:hw + tq] = cur_ref[:, k_off:k_off + kw]
    kbuf[hw + tq:] = hn_ref[:, 0:kw]
    vbuf[0:hw] = hp_ref[:, kw:2 * kw]
    vbuf[hw:hw + tq] = cur_ref[:, v_off:v_off + kw]
    vbuf[hw + tq:] = hn_ref[:, kw:2 * kw]
    for sb in range(tq // B_SUB):
        r0 = sb * B_SUB
        kpos = i * tq + (r0 - hw) + lax.broadcasted_iota(jnp.int32, (1, B_KEYS), 1)
        valid = jnp.where(kpos >= 0, kpos, seq_len) < seq_len
        for g in range(B_KV_HEADS):
            qs = jnp.concatenate(
                [cur_ref[r0:r0 + B_SUB, (g * B_REP + hh) * HEAD_DIM:(g * B_REP + hh + 1) * HEAD_DIM]
                 for hh in range(B_REP)], axis=0)
            k = kbuf[r0:r0 + B_KEYS, g * HEAD_DIM:(g + 1) * HEAD_DIM]
            v = vbuf[r0:r0 + B_KEYS, g * HEAD_DIM:(g + 1) * HEAD_DIM]
            s_all = lax.dot_general(qs, k, (((1,), (1,)), ((), ())), preferred_element_type=f32)
            s_all = s_all * SCALE + bias_ref[g]
            s_all = jnp.where(valid, s_all, NEG_INF)
            ps, dens = [], []
            for hh in range(B_REP):
                s = s_all[hh * B_SUB:(hh + 1) * B_SUB]
                sk = sink_ref[g * B_REP + hh]
                m = jnp.maximum(jnp.max(s, axis=-1, keepdims=True), sk)
                p = jnp.exp(s - m)
                dens.append(jnp.sum(p, axis=-1, keepdims=True) + jnp.exp(sk - m))
                ps.append(p.astype(bf16))
            o_all = jnp.dot(jnp.concatenate(ps, axis=0), v, preferred_element_type=f32)
            for hh in range(B_REP):
                c0 = (g * B_REP + hh) * HEAD_DIM
                o = o_all[hh * B_SUB:(hh + 1) * B_SUB] / dens[hh]
                o_ref[0, r0:r0 + B_SUB, c0:c0 + HEAD_DIM] = o.astype(o_ref.dtype)


def _attn_b(proj, bias, sink, bn, t):
    tq = 256
    nblk = t // tq
    hb = tq // B_HALF_WINDOW
    n_halo = t // B_HALF_WINDOW
    kvw = 2 * B_KV_HEADS * HEAD_DIM
    slab = N_AB_SLABS - 1
    pv = proj.reshape(N_AB_SLABS, bn, t, B_IN)
    out = pl.pallas_call(
        functools.partial(_attn_b_kernel, tq=tq, seq_len=t),
        grid=(bn, nblk),
        in_specs=[pl.BlockSpec(memory_space=pltpu.SMEM),
                  pl.BlockSpec((None, None, tq, B_IN), lambda b, i: (slab, b, i, 0)),
                  pl.BlockSpec((None, None, B_HALF_WINDOW, kvw),
                               lambda b, i: (slab, b, jnp.maximum(i * hb - 1, 0), B_IN // kvw - 1)),
                  pl.BlockSpec((None, None, B_HALF_WINDOW, kvw),
                               lambda b, i: (slab, b, jnp.minimum((i + 1) * hb, n_halo - 1), B_IN // kvw - 1)),
                  _resident((B_KV_HEADS, B_REP * B_SUB, B_KEYS), lambda b, i: (0, 0, 0))],
        out_specs=pl.BlockSpec((1, tq, B_OUT), lambda b, i: (b, i, 0)),
        out_shape=jax.ShapeDtypeStruct((bn, t, B_OUT), bf16),
        scratch_shapes=[pltpu.VMEM((tq + 2 * B_HALF_WINDOW, B_KV_HEADS * HEAD_DIM), bf16),
                        pltpu.VMEM((tq + 2 * B_HALF_WINDOW, B_KV_HEADS * HEAD_DIM), bf16)],
        compiler_params=_cparams(("parallel", "arbitrary")),
        name="attn_b",
    )(sink, pv, pv, pv, bias)
    return out.reshape(bn * t, B_OUT)


def _out_ab_kernel(x_ref, a0_ref, a1_ref, a2_ref, ob_ref, w_ref, o_ref, ilv_ref, mix_ref, *, tm):
    n_slabs = A_OUT_W // HEAD_DIM
    for gi, (a_ref, (_, d)) in enumerate(((a1_ref, DILATIONS[1]), (a2_ref, DILATIONS[2]))):
        per = tm // d
        for r in range(d):
            for s in range(n_slabs):
                ilv_ref[gi, s, pl.ds(r, per, stride=d), :] = a_ref[0, r, :, s * HEAD_DIM:(s + 1) * HEAD_DIM]
    for h in range(A_HEADS_PER_GROUP):
        c0 = h * HEAD_DIM
        outs = [a0_ref[:, c0:c0 + HEAD_DIM], ilv_ref[0, h], ilv_ref[1, h]]
        lses = [a0_ref[:, A_OUT + h:A_OUT + h + 1],
                ilv_ref[0, n_slabs - 1, :, h:h + 1],
                ilv_ref[1, n_slabs - 1, :, h:h + 1]]
        mx = jnp.maximum(jnp.maximum(lses[0], lses[1]), lses[2])
        es = [jnp.exp(l - mx) for l in lses]
        tot = es[0] + es[1] + es[2]
        acc = (es[0] / tot) * outs[0]
        acc += (es[1] / tot) * outs[1]
        acc += (es[2] / tot) * outs[2]
        mix_ref[:, c0:c0 + HEAD_DIM] = acc.astype(bf16)
    y = jnp.dot(ob_ref[...], w_ref[A_OUT:, :], preferred_element_type=f32)
    y += jnp.dot(mix_ref[...], w_ref[:A_OUT, :], preferred_element_type=f32)
    o_ref[...] = x_ref[...] + y


def _out_ab(x, a_outs, o_b, w, li, *, tm=512):
    m, d = x.shape
    row = lambda i: (i, 0)
    tiles = A_CHUNK // tm
    a0 = a_outs[0].reshape(m, A_OUT_W)
    dil = [dd for _, dd in DILATIONS]
    a1 = a_outs[1].reshape(m // A_CHUNK, dil[1], A_CHUNK // dil[1], A_OUT_W)
    a2 = a_outs[2].reshape(m // A_CHUNK, dil[2], A_CHUNK // dil[2], A_OUT_W)

    def deint(dd):
        return pl.BlockSpec((1, dd, tm // dd, A_OUT_W), lambda i: (i // tiles, 0, i % tiles, 0))

    return pl.pallas_call(
        functools.partial(_out_ab_kernel, tm=tm),
        grid=(m // tm,),
        in_specs=[pl.BlockSpec((tm, d), row),
                  pl.BlockSpec((tm, A_OUT_W), row),
                  deint(dil[1]),
                  deint(dil[2]),
                  pl.BlockSpec((tm, B_OUT), row),
                  _resident((None, AB_OUT, d), lambda i: (li, 0, 0))],
        out_specs=pl.BlockSpec((tm, d), row),
        out_shape=jax.ShapeDtypeStruct((m, d), f32),
        scratch_shapes=[pltpu.VMEM((2, A_OUT_W // HEAD_DIM, tm, HEAD_DIM), f32),
                        pltpu.VMEM((tm, A_OUT), bf16)],
        compiler_params=_cparams(("parallel",)),
        name="out_ab",
    )(x, a0, a1, a2, o_b, w)


def _na_bias(rpb):
    n_heads = rpb.shape[0]
    n_krows = C_KBLKS * C_QROWS
    c = np.arange(GRID_W)
    cs = np.clip(c - NA_COLS // 2, 0, GRID_W - NA_COLS)
    col_ok = (c[None, :] >= cs[:, None]) & (c[None, :] < cs[:, None] + NA_COLS)
    dc = np.clip(c[None, :] - c[:, None] + NA_COLS - 1, 0, 2 * NA_COLS - 2)
    onehot = (dc[None] == np.arange(2 * NA_COLS - 1)[:, None, None]).astype(np.float32)
    tiles = jnp.einsum('hdc,cqk->hqdk', rpb.astype(f32), onehot, precision=lax.Precision.HIGHEST)
    tiles = jnp.where(col_ok[None, :, None, :], tiles, NEG_INF)
    out = []
    for case in range(3):
        rows = []
        for qi in range(C_QROWS):
            first = (0, qi, C_QROWS)[case]
            dr0 = first - qi - case * C_QROWS + NA_ROWS - 1
            band = tiles[:, :, dr0:dr0 + NA_ROWS, :]
            rows.append(jnp.pad(band, ((0, 0), (0, 0), (first, n_krows - NA_ROWS - first), (0, 0)),
                                constant_values=NEG_INF))
        out.append(jnp.stack(rows, axis=1).reshape(n_heads, C_QTOK, n_krows * GRID_W))
    return jnp.stack(out) * LOG2E


def _attn_c_kernel(q_ref, k0_ref, k1_ref, k2_ref, v0_ref, v1_ref, v2_ref, bias_ref, o_ref):
    k_refs = (k0_ref, k1_ref, k2_ref)
    v_refs = (v0_ref, v1_ref, v2_ref)
    for h in range(C_HG):
        c0 = h * HEAD_DIM
        q = q_ref[0, :, c0:c0 + HEAD_DIM]
        ss = []
        for j in range(C_KBLKS):
            s = lax.dot_general(q, k_refs[j][0, :, c0:c0 + HEAD_DIM], (((1,), (1,)), ((), ())),
                                preferred_element_type=f32)
            ss.append(s * (SCALE * LOG2E) + bias_ref[0, h, :, j * C_QTOK:(j + 1) * C_QTOK])
        m = jnp.max(ss[0], axis=-1, keepdims=True)
        for j in range(1, C_KBLKS):
            m = jnp.maximum(m, jnp.max(ss[j], axis=-1, keepdims=True))
        den = None
        o = None
        for j in range(C_KBLKS):
            p = jnp.exp2(ss[j] - m)
            dj = jnp.sum(p, axis=-1, keepdims=True)
            oj = jnp.dot(p.astype(bf16), v_refs[j][0, :, c0:c0 + HEAD_DIM], preferred_element_type=f32)
            den = dj if den is None else den + dj
            o = oj if o is None else o + oj
        o_ref[0, :, c0:c0 + HEAD_DIM] = (o / den).astype(o_ref.dtype)


def _attn_c(qkv, bias):
    bn, t, _ = qkv.shape
    nblk = t // C_QTOK
    hgw = C_HG * HEAD_DIM
    n_hg = C_HEADS // C_HG
    k_cb = C_OUT // hgw
    v_cb = 2 * C_OUT // hgw

    def kv_spec(cb, j):
        return pl.BlockSpec((1, C_QTOK, hgw),
                            lambda g, b, i: (b, jnp.clip(i - 1, 0, nblk - C_KBLKS) + j, cb + g))

    def case(i):
        return jnp.where(i == 0, 0, jnp.where(i == nblk - 1, 2, 1))

    out = pl.pallas_call(
        _attn_c_kernel,
        grid=(n_hg, bn, nblk),
        in_specs=[pl.BlockSpec((1, C_QTOK, hgw), lambda g, b, i: (b, i, g)),
                  kv_spec(k_cb, 0), kv_spec(k_cb, 1), kv_spec(k_cb, 2),
                  kv_spec(v_cb, 0), kv_spec(v_cb, 1), kv_spec(v_cb, 2),
                  pl.BlockSpec((1, C_HG, C_QTOK, C_KBLKS * C_QTOK), lambda g, b, i: (case(i), g, 0, 0))],
        out_specs=pl.BlockSpec((1, C_QTOK, hgw), lambda g, b, i: (b, i, g)),
        out_shape=jax.ShapeDtypeStruct((bn, t, C_OUT), bf16),
        compiler_params=_cparams(("parallel", "parallel", "arbitrary")),
        name="attn_c",
    )(qkv, qkv, qkv, qkv, qkv, qkv, qkv, bias)
    return out.reshape(bn * t, C_OUT)


def _out_proj_kernel(x_ref, a_ref, w_ref, o_ref):
    o_ref[...] = x_ref[...] + jnp.dot(a_ref[...], w_ref[...], preferred_element_type=f32)


def _out_proj(x, a, w, li, *, tm=512):
    m, d = x.shape
    k = a.shape[1]
    row = lambda i: (i, 0)
    return pl.pallas_call(
        _out_proj_kernel,
        grid=(m // tm,),
        in_specs=[pl.BlockSpec((tm, d), row),
                  pl.BlockSpec((tm, k), row),
                  _resident((None, k, d), lambda i: (li, 0, 0))],
        out_specs=pl.BlockSpec((tm, d), row),
        out_shape=jax.ShapeDtypeStruct((m, d), f32),
        compiler_params=_cparams(("parallel",)),
        name="out_c",
    )(x, a, w)


def _mixer_ab(x, bn, t, g_norm, w_in, w_out, li, sink, a_biases, b_bias, perms):
    proj = _norm_proj_ab(x, g_norm, w_in, li, perms)
    a_outs = [_attn_a_group(proj, a_biases[g], g, d, bn, t) for g, (_, d) in enumerate(DILATIONS)]
    o_b = _attn_b(proj, b_bias, sink, bn, t)
    return _out_ab(x, a_outs, o_b, w_out, li)


def _mixer_c(x, bn, t, g_norm, w_in, w_out, li, c_bias):
    qkv = _norm_matmul(x, g_norm, w_in, li).reshape(bn, t, C_IN)
    return _out_proj(x, _attn_c(qkv, c_bias), w_out, li)


def _trunk(x3, w_in_ab, w_out_ab, sink_b, w_in_c, w_out_c, a_biases, b_bias, c_biases, perms,
           norm_mix, norm_ffn, w_gate, w_up, w_down, norm_final):
    bn, t, d = x3.shape
    x = x3.reshape(bn * t, d)
    for layer in range(DEPTH):
        j = layer // 2
        if layer % 2 == 0:
            x = _mixer_ab(x, bn, t, norm_mix[layer], w_in_ab, w_out_ab, j, sink_b[j], a_biases, b_bias, perms)
        else:
            x = _mixer_c(x, bn, t, norm_mix[layer], w_in_c, w_out_c, j, c_biases[j])
        x = _ffn(x, norm_ffn[layer], w_gate, w_up, w_down, layer,
                 norm_final if layer == DEPTH - 1 else None)
    return x.reshape(bn, t, d)


def _tables(t5_table, rpb_c):
    a_biases = []
    for g, (_, d) in enumerate(DILATIONS):
        cols = t5_table[:, g * A_HEADS_PER_GROUP:(g + 1) * A_HEADS_PER_GROUP]
        a_biases.append(_band_bias(cols, A_SUB, A_HALF, d))
    b_bias = _band_bias(t5_table[:, A_HEADS:], B_SUB, B_HALF_WINDOW, 1)
    b_bias = b_bias.reshape(B_KV_HEADS, B_REP * B_SUB, B_KEYS)
    c_biases = [_na_bias(rpb_c[j]) for j in range(rpb_c.shape[0])]
    perms = jnp.asarray(np.stack([_deinterleave_perm(d) for _, d in DILATIONS[1:]]), bf16)
    return a_biases, b_bias, c_biases, perms


def kernel(x_prompt, x_sample, w_in_ab, w_out_ab, sink_b, w_in_c, w_out_c, rpb_c, t5_table, norm_mix, norm_ffn, w_gate, w_up, w_down, norm_final):
    a_biases, b_bias, c_biases, perms = _tables(t5_table, rpb_c)
    weights = (w_in_ab.astype(bf16), w_out_ab.astype(bf16), sink_b, w_in_c.astype(bf16), w_out_c.astype(bf16),
               a_biases, b_bias, c_biases, perms, norm_mix, norm_ffn,
               w_gate.astype(bf16), w_up.astype(bf16), w_down.astype(bf16), norm_final)
    return (_trunk(x_prompt, *weights), _trunk(x_sample, *weights))
```
